```python
import jax, jax.numpy as jnp
from jax import lax
import numpy as np

D_MODEL = 1024
BATCH = 8
SEQ = 2048
DEPTH = 4
DEC_BATCH = 128
DEC_SEQ = 4
PAST_LEN = 16384
PAGE_SIZE = 128

N_META = 16
W_A = D_MODEL // 2
W_B = D_MODEL - W_A
A_HEADS = 8
A_HEAD_DIM = W_A // A_HEADS
CONV_W = 4
RG_C = 8.0
B_HEADS = 4
B_DK = W_B // B_HEADS
B_DV = W_B // B_HEADS
CHUNK = 64
D_FF = 4 * D_MODEL
IN_COLS = 2 * W_A + 4 * W_B
EPS = 1e-6
F_MIN = 1e-30

kernel_name = "hymba_rglru_hgrn2_decoder_step"


def _rms(x):
    x32 = x.astype(jnp.float32)
    return x32 * lax.rsqrt(jnp.mean(x32 * x32, axis=-1, keepdims=True) + EPS)


def _rmsnorm(x, g):
    return (_rms(x) * g.astype(jnp.float32)).astype(x.dtype)


def _lin_combine(c1, c2):
    a1, b1 = c1
    a2, b2 = c2
    return a1 * a2, a2 * b1 + b2


def _hgrn2_chunk(S, q, k, v, logf):
    C = q.shape[2]
    b = jnp.cumsum(logf, axis=2)
    inter = jnp.einsum('bhck,bhkv->bhcv', q * jnp.exp(b), S)
    diff = b[:, :, :, None, :] - b[:, :, None, :, :]
    mask = jnp.tril(jnp.ones((C, C), dtype=bool))[:, :, None]
    decay = jnp.exp(jnp.where(mask, diff, 0.0)) * mask.astype(diff.dtype)
    A = jnp.einsum('bhtk,bhtsk,bhsk->bhts', q, decay, k)
    intra = jnp.einsum('bhts,bhsv->bhtv', A, v)
    b_last = b[:, :, -1:, :]
    S_new = S * jnp.exp(b_last[:, :, 0, :])[..., None] + jnp.einsum(
        'bhsk,bhsv->bhkv', k * jnp.exp(b_last - b), v)
    return S_new, inter + intra


def _hgrn2(q, k, v, logf, S0, first_len):
    Bn, H, T, _ = q.shape
    S, o_first = _hgrn2_chunk(S0, q[:, :, :first_len], k[:, :, :first_len],
                              v[:, :, :first_len], logf[:, :, :first_len])
    rest = T - first_len
    if rest == 0:
        return o_first, S
    n = rest // CHUNK

    def split(z):
        return jnp.moveaxis(z[:, :, first_len:].reshape(Bn, H, n, CHUNK, z.shape[-1]), 2, 0)

    S, o_rest = lax.scan(lambda s, xs: _hgrn2_chunk(s, *xs), S,
                         (split(q), split(k), split(v), split(logf)))
    o_rest = jnp.moveaxis(o_rest, 0, 2).reshape(Bn, H, rest, v.shape[-1])
    return jnp.concatenate([o_first, o_rest], axis=2), S


def _mixer(hn, conv_buf, h0, S0, w_in, conv_w, conv_b, w_ra, b_ra, w_rx, b_rx,
           lam, lb, gn, w_out, first_len):
    Bn, T, _ = hn.shape
    f32 = jnp.float32
    u = (hn @ w_in).astype(f32)
    xa, ga, qb, fb, ib, gb = jnp.split(
        u, [W_A, 2 * W_A, 2 * W_A + W_B, 2 * W_A + 2 * W_B, 2 * W_A + 3 * W_B], axis=-1)
    xpad = jnp.concatenate([conv_buf.astype(f32), xa], axis=1)
    cw = conv_w.astype(f32)
    xc = conv_b.astype(f32) + sum(xpad[:, j:j + T] * cw[j] for j in range(CONV_W))
    new_conv = xpad[:, T:]
    xh = xc.reshape(Bn, T, A_HEADS, A_HEAD_DIM)
    r = jax.nn.sigmoid(jnp.einsum('bthi,hij->bthj', xh, w_ra.astype(f32)) + b_ra).reshape(Bn, T, W_A)
    ig = jax.nn.sigmoid(jnp.einsum('bthi,hij->bthj', xh, w_rx.astype(f32)) + b_rx).reshape(Bn, T, W_A)
    log_a = -RG_C * r * jax.nn.softplus(-lam.astype(f32))
    a = jnp.exp(log_a)
    bt = jnp.sqrt(-jnp.expm1(2.0 * log_a)) * ig * xc
    bt = bt.at[:, 0].add(a[:, 0] * h0.astype(f32))
    _, hs = lax.associative_scan(_lin_combine, (a, bt), axis=1)
    y_a = hs * jax.nn.gelu(ga)
    h_new = hs[:, -1]
    def heads(z):
        return z.reshape(Bn, T, B_HEADS, -1).transpose(0, 2, 1, 3)
    q = heads(jax.nn.silu(qb)) * (B_DK ** -0.5)
    lbh = lb.astype(f32).reshape(B_HEADS, 1, B_DK)
    fpre = heads(fb)
    f = lbh + (1.0 - lbh) * jax.nn.sigmoid(fpre)
    logf = jnp.log(jnp.maximum(f, F_MIN))
    k = (1.0 - lbh) * jax.nn.sigmoid(-fpre)
    v = heads(ib)
    o, S_new = _hgrn2(q, k, v, logf, S0.astype(f32), first_len)
    o = o.transpose(0, 2, 1, 3)
    o = _rms(o) * gn.astype(f32) * jax.nn.silu(gb.reshape(Bn, T, B_HEADS, B_DV))
    y = jnp.concatenate([y_a, o.reshape(Bn, T, W_B)], axis=-1)
    return (y.astype(hn.dtype) @ w_out), new_conv, h_new, S_new


def setup_inputs(seed: int = 0) -> dict:
    key = jax.random.key(seed)
    ks = jax.random.split(key, 24)
    nrm = jax.random.normal
    f32 = jnp.float32
    u_a = jax.random.uniform(ks[13], (DEPTH, W_A), f32, 0.9, 0.999)
    return {
        "x_prompt": nrm(ks[0], (BATCH, SEQ, D_MODEL), f32),
        "x_sample": nrm(ks[1], (DEC_BATCH, DEC_SEQ, D_MODEL), f32),
        "state_rglru_h": 0.5 * nrm(ks[2], (DEPTH, DEC_BATCH, W_A), f32),
        "state_rglru_conv": nrm(ks[3], (DEPTH, DEC_BATCH, CONV_W - 1, W_A), f32),
        "state_hgrn2": 0.3 * nrm(ks[4], (DEPTH, DEC_BATCH, B_HEADS, B_DK, B_DV), f32),
        "meta_tokens": nrm(ks[5], (N_META, D_MODEL), f32),
        "ln_mix": 1.0 + 0.02 * nrm(ks[6], (DEPTH, D_MODEL), f32),
        "w_in": nrm(ks[7], (DEPTH, D_MODEL, IN_COLS), f32) * D_MODEL ** -0.5,
        "conv_w": nrm(ks[8], (DEPTH, CONV_W, W_A), f32) * CONV_W ** -0.5,
        "conv_b": 0.02 * nrm(ks[9], (DEPTH, W_A), f32),
        "w_rg_a": nrm(ks[10], (DEPTH, A_HEADS, A_HEAD_DIM, A_HEAD_DIM), f32) * A_HEAD_DIM ** -0.5,
        "b_rg_a": 0.02 * nrm(ks[11], (DEPTH, A_HEADS, A_HEAD_DIM), f32),
        "w_rg_x": nrm(ks[12], (DEPTH, A_HEADS, A_HEAD_DIM, A_HEAD_DIM), f32) * A_HEAD_DIM ** -0.5,
        "b_rg_x": 0.02 * nrm(ks[14], (DEPTH, A_HEADS, A_HEAD_DIM), f32),
        "rg_lambda": jnp.log(u_a / (1.0 - u_a)),
        "hgrn_lb": 0.5 * nrm(ks[15], (DEPTH, W_B), f32),
        "hgrn_norm": 1.0 + 0.02 * nrm(ks[16], (DEPTH, B_DV), f32),
        "w_out": nrm(ks[17], (DEPTH, D_MODEL, D_MODEL), f32) * D_MODEL ** -0.5,
        "ln_mlp": 1.0 + 0.02 * nrm(ks[18], (DEPTH, D_MODEL), f32),
        "w_up": nrm(ks[19], (DEPTH, D_MODEL, D_FF), f32) * D_MODEL ** -0.5,
        "w_down": nrm(ks[20], (DEPTH, D_FF, D_MODEL), f32) * D_FF ** -0.5,
        "ln_final": 1.0 + 0.02 * nrm(ks[21], (D_MODEL,), f32),
    }


def reference(x_prompt, x_sample, state_rglru_h, state_rglru_conv, state_hgrn2,
              meta_tokens, ln_mix, w_in, conv_w, conv_b, w_rg_a, b_rg_a, w_rg_x, b_rg_x,
              rg_lambda, hgrn_lb, hgrn_norm, w_out, ln_mlp, w_up, w_down, ln_final):
    f32 = jnp.float32
    lbs = jax.nn.softmax(hgrn_lb.astype(f32), axis=0)
    lb_all = jnp.clip(jnp.cumsum(lbs, axis=0) - lbs[0], 0.0, 1.0)

    meta = jnp.broadcast_to(meta_tokens.astype(x_prompt.dtype), (BATCH, N_META, D_MODEL))
    xp = jnp.concatenate([meta, x_prompt], axis=1)
    xs = x_sample
    p_conv = jnp.zeros((BATCH, CONV_W - 1, W_A), f32)
    p_h = jnp.zeros((BATCH, W_A), f32)
    p_S = jnp.zeros((BATCH, B_HEADS, B_DK, B_DV), f32)

    ph_l, pc_l, pS_l, sh_l, sc_l, sS_l = [], [], [], [], [], []
    for l in range(DEPTH):
        wl = (w_in[l], conv_w[l], conv_b[l], w_rg_a[l], b_rg_a[l], w_rg_x[l], b_rg_x[l],
              rg_lambda[l], lb_all[l], hgrn_norm[l], w_out[l])
        mp, cp, hp, Sp = _mixer(_rmsnorm(xp, ln_mix[l]), p_conv, p_h, p_S, *wl, N_META)
        ms, cs, hs, Ss = _mixer(_rmsnorm(xs, ln_mix[l]), state_rglru_conv[l], state_rglru_h[l],
                                state_hgrn2[l], *wl, xs.shape[1])
        xp = xp + mp
        xs = xs + ms
        pn = _rmsnorm(xp, ln_mlp[l])
        xp = xp + jnp.square(jax.nn.relu(pn @ w_up[l])) @ w_down[l]
        sn = _rmsnorm(xs, ln_mlp[l])
        xs = xs + jnp.square(jax.nn.relu(sn @ w_up[l])) @ w_down[l]
        ph_l.append(hp); pc_l.append(cp); pS_l.append(Sp)
        sh_l.append(hs); sc_l.append(cs); sS_l.append(Ss)

    y_prompt = _rmsnorm(xp, ln_final)[:, N_META:]
    y_sample = _rmsnorm(xs, ln_final)
    return (y_prompt, y_sample, jnp.stack(ph_l), jnp.stack(pc_l), jnp.stack(pS_l),
            jnp.stack(sh_l), jnp.stack(sc_l), jnp.stack(sS_l))
```

```python
import functools

import jax
import jax.numpy as jnp
import numpy as np
from jax import lax
from jax.experimental import pallas as pl
from jax.experimental.pallas import tpu as pltpu

N_META = 16
CONV_W = 4
A_HEADS = 8
B_HEADS = 4
RG_C = 8.0
EPS = 1e-6
F_MIN = 1e-30

LANES = 128
SUBLANES = 8
EXACT = 8
PROMPT_CHUNK = 64
SAMPLE_GROUP = 8
MLP_ROWS = 512
MLP_COLS = 1024
VMEM_LIMIT = 56 * 1024 * 1024

_BF = jnp.bfloat16
_F32 = jnp.float32
_NT = (((1,), (1,)), ((), ()))
_TN = (((0,), (0,)), ((), ()))


def _rms_rows(x, g):
    ms = jnp.mean(x * x, axis=-1, keepdims=True)
    return x * lax.rsqrt(ms + EPS) * g


def _sigmoid_pair(x):
    e = jnp.exp(-jnp.abs(x))
    inv = 1.0 / (1.0 + e)
    big, small = inv, e * inv
    pos = x >= 0
    return jnp.where(pos, big, small), jnp.where(pos, small, big)


def _pad_rows(x, rows):
    if x.shape[0] == rows:
        return x
    return jnp.concatenate([x, jnp.zeros((rows - x.shape[0], x.shape[1]), x.dtype)], axis=0)


def _level_list(chunk):
    out, lv = [], EXACT
    while 2 * lv <= chunk:
        out.append(lv)
        lv *= 2
    return tuple(out)


def _level_masks(chunk):
    lv = _level_list(chunk)
    t = np.arange(chunk)[:, None]
    s = np.arange(chunk)[None, :]
    ms = [((t // (2 * L) == s // (2 * L)) & (t % (2 * L) >= L) & (s % (2 * L) < L)) for L in lv]
    if not ms:
        return np.zeros((1, SUBLANES, LANES), np.float32)
    return np.stack(ms).astype(np.float32)


def _mixer_kernel(x_ref, h0_ref, c0_ref, s0_ref, lnm_ref, win_ref, cw_ref, cb_ref, wra_ref, bra_ref,
                  wrx_ref, brx_ref, lam_ref, lb_ref, gn_ref, wout_ref, mask_ref,
                  xo_ref, h_ref, c_ref, s_ref,
                  u_sc, xp_sc, a_sc, hs_sc, q_sc, k_sc, b_sc, od_sc,
                  qe_sc, kb_sc, v_sc, m_sc, o_sc, eb_sc, ebt_sc,
                  *, NB, C, cx, levels, tiny, G):
    W = h_ref.shape[-1]
    TB = C * NB
    NS = C // cx
    i = pl.program_id(0)

    def front():
        x = x_ref[...]
        hn = _rms_rows(x, lnm_ref[...]).astype(_BF)
        u_sc[...] = jnp.dot(hn, win_ref[...], preferred_element_type=_F32)

        xp_sc[3:] = u_sc[:, 0:W].reshape(C, NB, W)
        cw = cw_ref[...]
        xc = (cb_ref[...] + xp_sc[3:C + 3] * cw[3:4] + xp_sc[2:C + 2] * cw[2:3]
              + xp_sc[1:C + 1] * cw[1:2] + xp_sc[0:C] * cw[0:1])
        c_ref[...] = xp_sc[C:C + 3]
        xc2 = xc.reshape(TB, W)
        xcb = xc2.astype(_BF)
        r = jax.nn.sigmoid(jnp.dot(xcb, wra_ref[...], preferred_element_type=_F32) + bra_ref[...])
        ig = jax.nn.sigmoid(jnp.dot(xcb, wrx_ref[...], preferred_element_type=_F32) + brx_ref[...])
        lam = lam_ref[...]
        sp = jnp.maximum(-lam, 0.0) + jnp.log1p(jnp.exp(-jnp.abs(lam)))
        a = jnp.exp((-RG_C) * r * sp)
        bt = jnp.sqrt(1.0 - a * a) * ig * xc2
        a_sc[...] = a.reshape(C, NB, W)
        hs_sc[...] = bt.reshape(C, NB, W)
        h = h_ref[...]
        for t in range(C):
            h = a_sc[t] * h + hs_sc[t]
            hs_sc[t] = h
        h_ref[...] = h

        lb = lb_ref[...]
        qb = u_sc[:, 2 * W:3 * W]
        q = qb * jax.nn.sigmoid(qb) * (LANES ** -0.5)
        sg, sgn = _sigmoid_pair(u_sc[:, 3 * W:4 * W])
        f = lb + (1.0 - lb) * sg
        k = (1.0 - lb) * sgn
        logf = jnp.log(jnp.maximum(f, F_MIN))
        q_sc[...] = q.reshape(NS, cx, NB, W)
        k_sc[...] = k.reshape(NS, cx, NB, W)
        b_sc[...] = logf.reshape(NS, cx, NB, W)
        b = jnp.zeros((NB, W), _F32)
        for t in range(C):
            b = b + b_sc[t // cx, t % cx]
            b_sc[t // cx, t % cx] = b
        b_last = b
        vv = u_sc[:, 4 * W:5 * W]
        b2 = b_sc[...].reshape(TB, W)
        q2 = q_sc[...].reshape(TB, W)
        k2 = k_sc[...].reshape(TB, W)
        qe = q2 * jnp.exp(b2)
        kb = (k2.reshape(C, NB, W) * jnp.exp(b_last[None] - b2.reshape(C, NB, W))).reshape(TB, W)
        eb = jnp.exp(b_last)
        for hh in range(B_HEADS):
            sl = slice(hh * LANES, (hh + 1) * LANES)
            qe_sc[hh] = qe[:, sl]
            kb_sc[hh] = kb[:, sl]
            v_sc[hh] = vv[:, sl]
            eb_sc[hh] = eb[:, sl]
            if tiny:
                ebt_sc[hh] = eb[:, sl].T

        for li, L in enumerate(levels):
            spb = (2 * L) // cx
            for blk in range(C // (2 * L)):
                mid = blk * 2 * L + L - 1
                rmid = b_sc[mid // cx, mid % cx][None]
                bb = b_sc[blk * spb:(blk + 1) * spb].reshape(2 * L, NB, W)
                kk = k_sc[blk * spb:(blk + 1) * spb].reshape(2 * L, NB, W)
                qq = q_sc[blk * spb:(blk + 1) * spb].reshape(2 * L, NB, W)
                lo = (kk[:L] * jnp.exp(rmid - bb[:L])).reshape(L * NB, W)
                hi = (qq[L:] * jnp.exp(bb[L:] - rmid)).reshape(L * NB, W)
                r0 = blk * 2 * L * NB
                for hh in range(B_HEADS):
                    sl = slice(hh * LANES, (hh + 1) * LANES)
                    m_sc[li, hh, r0:r0 + L * NB, :] = lo[:, sl]
                    m_sc[li, hh, r0 + L * NB:r0 + 2 * L * NB, :] = hi[:, sl]

        v4 = vv.reshape(NS, cx, NB, W)
        for t in range(cx):
            acc = [jnp.zeros((NS, NB, LANES), _F32) for _ in range(B_HEADS)]
            for j in range(t + 1):
                p = q_sc[:, t] * k_sc[:, j]
                if j < t:
                    p = p * jnp.exp(b_sc[:, t] - b_sc[:, j])
                vj = v4[:, j]
                for hh in range(B_HEADS):
                    sl = slice(hh * LANES, (hh + 1) * LANES)
                    acc[hh] = acc[hh] + jnp.sum(p[..., sl], axis=-1, keepdims=True) * vj[..., sl]
            od_sc[:, t] = jnp.concatenate(acc, axis=-1)

    def chunk_state():
        def seq_body(bi, carry):
            rows = pl.ds(bi, C, stride=NB)
            for hh in range(B_HEADS):
                st = s_ref[bi, hh]
                qe = qe_sc[hh, rows, :].astype(_BF)
                kb = kb_sc[hh, rows, :].astype(_BF)
                v = v_sc[hh, rows, :].astype(_BF)
                o = lax.dot_general(qe, st.astype(_BF), _NT, preferred_element_type=_F32)
                if levels:
                    amat = jnp.zeros((C, C), _F32)
                    for li in range(len(levels)):
                        m = m_sc[li, hh, rows, :].astype(_BF)
                        amat = amat + lax.dot_general(m, m, _NT, preferred_element_type=_F32) * mask_ref[li]
                    o = o + jnp.dot(amat.astype(_BF), v, preferred_element_type=_F32)
                o_sc[hh, rows, :] = o
                upd = lax.dot_general(v, kb, _TN, preferred_element_type=_F32)
                s_ref[bi, hh] = st * eb_sc[hh, pl.ds(bi, 1), :] + upd
            return carry
        lax.fori_loop(0, NB, seq_body, 0)

    def tiny_state():
        lane = lax.broadcasted_iota(jnp.int32, (LANES, NB), 1)

        def seq_body(j, carry):
            n = i * G + j
            rows = pl.ds(n, C, stride=NB)
            onehot = lane == n
            for hh in range(B_HEADS):
                st = s0_ref[j, hh]
                qe = _pad_rows(qe_sc[hh, rows, :], 2 * SUBLANES).astype(_BF)
                kb = _pad_rows(kb_sc[hh, rows, :], 2 * SUBLANES).astype(_BF)
                v = _pad_rows(v_sc[hh, rows, :], 2 * SUBLANES).astype(_BF)
                o = jnp.dot(qe, st.astype(_BF), preferred_element_type=_F32)
                o_sc[hh, rows, :] = o[:C]
                ebcol = jnp.sum(jnp.where(onehot, ebt_sc[hh], 0.0), axis=1, keepdims=True)
                upd = lax.dot_general(kb, v, _TN, preferred_element_type=_F32)
                s_ref[j, hh] = st * ebcol + upd
            return carry
        lax.fori_loop(0, G, seq_body, 0)

    def back():
        og = []
        od = od_sc[...].reshape(TB, W)
        for hh in range(B_HEADS):
            sl = slice(hh * LANES, (hh + 1) * LANES)
            o = o_sc[hh] + od[:, sl]
            gb = u_sc[:, 5 * W + hh * LANES:5 * W + (hh + 1) * LANES]
            ms = jnp.mean(o * o, axis=-1, keepdims=True)
            og.append(o * lax.rsqrt(ms + EPS) * gn_ref[...] * (gb * jax.nn.sigmoid(gb)))
        ya = hs_sc[...].reshape(TB, W) * jax.nn.gelu(u_sc[:, W:2 * W])
        y = jnp.concatenate([ya] + og, axis=-1).astype(_BF)
        xo_ref[...] = x_ref[...] + jnp.dot(y, wout_ref[...], preferred_element_type=_F32)

    if tiny:
        @pl.when(i == 0)
        def _():
            h_ref[...] = h0_ref[...]
            xp_sc[0:3] = c0_ref[...]
            front()

        tiny_state()

        @pl.when(i == pl.num_programs(0) - 1)
        def _():
            back()
    else:
        @pl.when(i == 0)
        def _():
            h_ref[...] = h0_ref[...]
            c_ref[...] = c0_ref[...]
            s_ref[...] = s0_ref[...]

        xp_sc[0:3] = c_ref[...]
        front()
        chunk_state()
        back()


def _const_spec(shape, single=True):
    nd = len(shape)
    if single:
        return pl.BlockSpec(shape, lambda i, _n=nd: (0,) * _n, pipeline_mode=pl.Buffered(1))
    return pl.BlockSpec(shape, lambda i, _n=nd: (0,) * _n)


def _mixer(x, h0, c0, s0, lw, *, NB, C, tiny):
    rows, D = x.shape
    W = D // 2
    TB = C * NB
    cx = min(EXACT, C)
    levels = _level_list(C)
    if tiny:
        G = min(SAMPLE_GROUP, NB)
        grid = (NB // G,)
        x_spec = _const_spec((TB, D))
        s_spec = pl.BlockSpec((G, B_HEADS, LANES, LANES), lambda i: (i, 0, 0, 0))
        assert rows == TB
    else:
        G = NB
        grid = (rows // TB,)
        x_spec = pl.BlockSpec((TB, D), lambda i: (i, 0))
        s_spec = _const_spec((NB, B_HEADS, LANES, LANES))
    masks = jnp.asarray(_level_masks(C))
    nl = max(len(levels), 1)
    weights = (lw["ln_mix"], lw["w_in"], lw["conv_w"], lw["conv_b"], lw["w_ra"], lw["b_ra"],
               lw["w_rx"], lw["b_rx"], lw["lam"], lw["lb"], lw["gn"], lw["w_out"], masks)
    in_specs = [x_spec, _const_spec(h0.shape), _const_spec(c0.shape), s_spec]
    in_specs += [_const_spec(w.shape) for w in weights]
    out_shape = (jax.ShapeDtypeStruct((rows, D), _F32), jax.ShapeDtypeStruct((NB, W), _F32),
                 jax.ShapeDtypeStruct((3, NB, W), _F32),
                 jax.ShapeDtypeStruct((NB, B_HEADS, LANES, LANES), _F32))
    xo_spec = _const_spec((TB, D), single=False) if tiny else x_spec
    so_spec = s_spec if tiny else _const_spec((NB, B_HEADS, LANES, LANES), single=False)
    out_specs = (xo_spec, _const_spec((NB, W), single=False), _const_spec((3, NB, W), single=False), so_spec)
    NS = C // cx
    scratch = [
        pltpu.VMEM((TB, 6 * W), _F32),
        pltpu.VMEM((C + 3, NB, W), _F32),
        pltpu.VMEM((C, NB, W), _F32),
        pltpu.VMEM((C, NB, W), _F32),
        pltpu.VMEM((NS, cx, NB, W), _F32),
        pltpu.VMEM((NS, cx, NB, W), _F32),
        pltpu.VMEM((NS, cx, NB, W), _F32),
        pltpu.VMEM((NS, cx, NB, W), _F32),
        pltpu.VMEM((B_HEADS, TB, LANES), _F32),
        pltpu.VMEM((B_HEADS, TB, LANES), _F32),
        pltpu.VMEM((B_HEADS, TB, LANES), _F32),
        pltpu.VMEM((nl, B_HEADS, TB, LANES), _F32),
        pltpu.VMEM((B_HEADS, TB, LANES), _F32),
        pltpu.VMEM((B_HEADS, NB, LANES), _F32),
        pltpu.VMEM((B_HEADS, LANES, NB if tiny else LANES), _F32),
    ]
    kern = functools.partial(_mixer_kernel, NB=NB, C=C, cx=cx, levels=levels, tiny=tiny, G=G)
    return pl.pallas_call(
        kern, grid=grid, in_specs=in_specs, out_specs=out_specs, out_shape=out_shape,
        scratch_shapes=scratch,
        compiler_params=pltpu.CompilerParams(dimension_semantics=("arbitrary",),
                                             vmem_limit_bytes=VMEM_LIMIT),
        name="mixer_tiny" if tiny else "mixer_chunk",
    )(x, h0, c0, s0, *weights)


def _mlp_kernel(x_ref, ln_ref, wup_ref, wdn_ref, lnf_ref, o_ref, *, final):
    x = x_ref[...]
    pn = _rms_rows(x, ln_ref[...]).astype(_BF)
    acc = x
    F = wup_ref.shape[1]
    for c0 in range(0, F, MLP_COLS):
        hmid = jnp.dot(pn, wup_ref[:, c0:c0 + MLP_COLS], preferred_element_type=_F32)
        hmid = jnp.square(jnp.maximum(hmid, 0.0)).astype(_BF)
        acc = acc + jnp.dot(hmid, wdn_ref[c0:c0 + MLP_COLS, :], preferred_element_type=_F32)
    if final:
        acc = _rms_rows(acc, lnf_ref[...])
    o_ref[...] = acc


def _mlp(x, lw, lnf, *, final):
    rows, D = x.shape
    TM = min(MLP_ROWS, rows)
    assert rows % TM == 0
    weights = (lw["ln_mlp"], lw["w_up"], lw["w_down"], lnf)
    x_spec = pl.BlockSpec((TM, D), lambda i: (i, 0))
    return pl.pallas_call(
        functools.partial(_mlp_kernel, final=final), grid=(rows // TM,),
        in_specs=[x_spec] + [_const_spec(w.shape) for w in weights],
        out_specs=x_spec, out_shape=jax.ShapeDtypeStruct((rows, D), _F32),
        compiler_params=pltpu.CompilerParams(dimension_semantics=("arbitrary",),
                                             vmem_limit_bytes=VMEM_LIMIT),
        name="mlp",
    )(x, *weights)


def _block_diag(w):
    H, I, J = w.shape
    eye = jnp.eye(H, dtype=w.dtype)
    return (w[:, :, None, :] * eye[:, None, :, None]).reshape(H * I, H * J)


def kernel(x_prompt, x_sample, state_rglru_h, state_rglru_conv, state_hgrn2, meta_tokens, ln_mix, w_in,
           conv_w, conv_b, w_rg_a, b_rg_a, w_rg_x, b_rg_x, rg_lambda, hgrn_lb, hgrn_norm, w_out, ln_mlp,
           w_up, w_down, ln_final):
    B, T, D = x_prompt.shape
    NS, TS, _ = x_sample.shape
    depth = w_in.shape[0]
    W = D // 2
    assert B == SUBLANES and W == B_HEADS * LANES and T % PROMPT_CHUNK == 0

    lbs = jax.nn.softmax(hgrn_lb.astype(_F32), axis=0)
    lb_all = jnp.clip(jnp.cumsum(lbs, axis=0) - lbs[0], 0.0, 1.0)

    xp = jnp.swapaxes(x_prompt, 0, 1).reshape(T * B, D)
    xs = jnp.swapaxes(x_sample, 0, 1).reshape(TS * NS, D)
    xm = jnp.broadcast_to(meta_tokens.astype(_F32)[:, None, :], (N_META, B, D)).reshape(N_META * B, D)
    conv_s = jnp.swapaxes(state_rglru_conv, 1, 2)
    lnf = ln_final.reshape(1, D)

    zero_h = jnp.zeros((B, W), _F32)
    zero_c = jnp.zeros((CONV_W - 1, B, W), _F32)
    zero_s = jnp.zeros((B, B_HEADS, LANES, LANES), _F32)

    outs = {k: [] for k in ("ph", "pc", "pS", "sh", "sc", "sS")}
    for l in range(depth):
        lw = dict(
            ln_mix=ln_mix[l].reshape(1, D), w_in=w_in[l].astype(_BF), conv_w=conv_w[l],
            conv_b=conv_b[l].reshape(1, W), w_ra=_block_diag(w_rg_a[l]).astype(_BF),
            b_ra=b_rg_a[l].reshape(1, W), w_rx=_block_diag(w_rg_x[l]).astype(_BF),
            b_rx=b_rg_x[l].reshape(1, W), lam=rg_lambda[l].reshape(1, W), lb=lb_all[l].reshape(1, W),
            gn=hgrn_norm[l].reshape(1, LANES), w_out=w_out[l].astype(_BF),
            ln_mlp=ln_mlp[l].reshape(1, D), w_up=w_up[l].astype(_BF), w_down=w_down[l].astype(_BF))
        final = l == depth - 1
        xm, hm, cm, sm = _mixer(xm, zero_h, zero_c, zero_s, lw, NB=B, C=N_META, tiny=False)
        if not final:
            xm = _mlp(xm, lw, lnf, final=False)
        xp, hp, cp, sp = _mixer(xp, hm, cm, sm, lw, NB=B, C=PROMPT_CHUNK, tiny=False)
        xp = _mlp(xp, lw, lnf, final=final)
        xs, hs, cs, ss = _mixer(xs, state_rglru_h[l], conv_s[l], state_hgrn2[l], lw, NB=NS, C=TS, tiny=True)
        xs = _mlp(xs, lw, lnf, final=final)
        outs["ph"].append(hp)
        outs["pc"].append(jnp.swapaxes(cp, 0, 1))
        outs["pS"].append(jnp.swapaxes(sp, -1, -2))
        outs["sh"].append(hs)
        outs["sc"].append(jnp.swapaxes(cs, 0, 1))
        outs["sS"].append(ss)

    y_prompt = jnp.swapaxes(xp.reshape(T, B, D), 0, 1)
    y_sample = jnp.swapaxes(xs.reshape(TS, NS, D), 0, 1)
    return (y_prompt, y_sample, jnp.stack(outs["ph"]), jnp.stack(outs["pc"]), jnp.stack(outs["pS"]),
            jnp.stack(outs["sh"]), jnp.stack(outs["sc"]), jnp.stack(outs["sS"]))
```

```python
import functools

import jax
import jax.numpy as jnp
import numpy as np
from jax import lax
from jax.experimental import pallas as pl
from jax.experimental.pallas import tpu as pltpu

N_META = 16
CONV_W = 4
A_HEADS = 8
B_HEADS = 4
RG_C = 8.0
EPS = 1e-6
F_MIN = 1e-30
LOG2E = 1.4426950408889634

LANES = 128
SUBLANES = 8
EXACT = 8
PROMPT_CHUNK = 64
SAMPLE_GROUP = 8
MLP_ROWS = 512
MLP_COLS = 1024
VMEM_LIMIT = 56 * 1024 * 1024

_BF = jnp.bfloat16
_F32 = jnp.float32
_NT = (((1,), (1,)), ((), ()))
_TN = (((0,), (0,)), ((), ()))


def _rms_rows(x, g):
    ms = jnp.mean(x * x, axis=-1, keepdims=True)
    return x * lax.rsqrt(ms + EPS) * g


def _sigmoid_pair(x):
    e = jnp.exp(-jnp.abs(x))
    inv = 1.0 / (1.0 + e)
    big, small = inv, e * inv
    pos = x >= 0
    return jnp.where(pos, big, small), jnp.where(pos, small, big)


def _pad_rows(x, rows):
    if x.shape[0] == rows:
        return x
    return jnp.concatenate([x, jnp.zeros((rows - x.shape[0], x.shape[1]), x.dtype)], axis=0)


def _level_list(chunk):
    out, lv = [], EXACT
    while 2 * lv <= chunk:
        out.append(lv)
        lv *= 2
    return tuple(out)


def _level_masks(chunk):
    lv = _level_list(chunk)
    t = np.arange(chunk)[:, None]
    s = np.arange(chunk)[None, :]
    ms = [((t // (2 * L) == s // (2 * L)) & (t % (2 * L) >= L) & (s % (2 * L) < L)) for L in lv]
    if not ms:
        return np.zeros((1, SUBLANES, LANES), np.float32)
    return np.stack(ms).astype(np.float32)


def _mixer_kernel(x_ref, h0_ref, c0_ref, s0_ref, lnm_ref, win_ref, cw_ref, cb_ref, wra_ref,
                  bra_ref, wrx_ref, brx_ref, lam_ref, lb_ref, gn_ref, wout_ref, mask_ref,
                  xo_ref, h_ref, c_ref, s_ref,
                  u_sc, xp_sc, a_sc, hs_sc, q_sc, k_sc, b_sc, od_sc,
                  qe_sc, kb_sc, v_sc, m_sc, o_sc, eb_sc, ebt_sc, am_sc,
                  *, NB, C, cx, levels, tiny, G):
    W = h_ref.shape[-1]
    TB = C * NB
    NS = C // cx
    i = pl.program_id(0)

    def front():
        hn = _rms_rows(x_ref[...], lnm_ref[...]).astype(_BF)

        def proj(g):
            u_sc[:, g * W:(g + 1) * W] = jnp.dot(hn, win_ref[:, g * W:(g + 1) * W],
                                                 preferred_element_type=_F32)

        proj(0)
        proj(3)
        proj(2)
        xp_sc[3:] = u_sc[:, 0:W].reshape(C, NB, W)
        cw = cw_ref[...]
        xc = (cb_ref[...] + xp_sc[3:C + 3] * cw[3:4] + xp_sc[2:C + 2] * cw[2:3]
              + xp_sc[1:C + 1] * cw[1:2] + xp_sc[0:C] * cw[0:1])
        c_ref[...] = xp_sc[C:C + 3]
        xc2 = xc.reshape(TB, W)
        xcb = xc2.astype(_BF)
        r = jax.nn.sigmoid(jnp.dot(xcb, wra_ref[...], preferred_element_type=_F32) + bra_ref[...])
        ig = jax.nn.sigmoid(jnp.dot(xcb, wrx_ref[...], preferred_element_type=_F32) + brx_ref[...])
        proj(4)
        proj(5)
        proj(1)
        lam = lam_ref[...]
        sp = jnp.maximum(-lam, 0.0) + jnp.log1p(jnp.exp(-jnp.abs(lam)))
        a = jnp.exp((-RG_C) * r * sp)
        om = 1.0 - a * a
        bt = jnp.where(om > 0.0, om * lax.rsqrt(om), 0.0) * ig * xc2
        a_sc[...] = a.reshape(C, NB, W)
        hs_sc[...] = bt.reshape(C, NB, W)
        h = h_ref[...]
        for t in range(C):
            h = a_sc[t] * h + hs_sc[t]
            hs_sc[t] = h
        h_ref[...] = h

        lb = lb_ref[...]
        qb = u_sc[:, 2 * W:3 * W]
        q = qb * jax.nn.sigmoid(qb) * (LANES ** -0.5)
        sg, sgn = _sigmoid_pair(u_sc[:, 3 * W:4 * W])
        f = lb + (1.0 - lb) * sg
        k = (1.0 - lb) * sgn
        logf = jnp.log(jnp.maximum(f, F_MIN)) * LOG2E
        q_sc[...] = q.reshape(NS, cx, NB, W)
        k_sc[...] = k.reshape(NS, cx, NB, W)
        b_sc[...] = logf.reshape(NS, cx, NB, W)
        b = jnp.zeros((NB, W), _F32)
        for t in range(C):
            b = b + b_sc[t // cx, t % cx]
            b_sc[t // cx, t % cx] = b
        b_last = b
        vv = u_sc[:, 4 * W:5 * W]
        b2 = b_sc[...].reshape(TB, W)
        q2 = q_sc[...].reshape(TB, W)
        k2 = k_sc[...].reshape(TB, W)
        qe = q2 * jnp.exp2(b2)
        kb = (k2.reshape(C, NB, W) * jnp.exp2(b_last[None] - b2.reshape(C, NB, W))).reshape(TB, W)
        eb = jnp.exp2(b_last)
        for hh in range(B_HEADS):
            sl = slice(hh * LANES, (hh + 1) * LANES)
            qe_sc[hh] = qe[:, sl]
            kb_sc[hh] = kb[:, sl]
            v_sc[hh] = vv[:, sl]
            eb_sc[hh] = eb[:, sl]
            if tiny:
                ebt_sc[hh] = eb[:, sl].T

        for li, L in enumerate(levels):
            spb = (2 * L) // cx
            for blk in range(C // (2 * L)):
                mid = blk * 2 * L + L - 1
                rmid = b_sc[mid // cx, mid % cx][None]
                bb = b_sc[blk * spb:(blk + 1) * spb].reshape(2 * L, NB, W)
                kk = k_sc[blk * spb:(blk + 1) * spb].reshape(2 * L, NB, W)
                qq = q_sc[blk * spb:(blk + 1) * spb].reshape(2 * L, NB, W)
                lo = (kk[:L] * jnp.exp2(rmid - bb[:L])).reshape(L * NB, W)
                hi = (qq[L:] * jnp.exp2(bb[L:] - rmid)).reshape(L * NB, W)
                r0 = blk * 2 * L * NB
                for hh in range(B_HEADS):
                    sl = slice(hh * LANES, (hh + 1) * LANES)
                    m_sc[li, hh, r0:r0 + L * NB, :] = lo[:, sl]
                    m_sc[li, hh, r0 + L * NB:r0 + 2 * L * NB, :] = hi[:, sl]

        v4 = vv.reshape(NS, cx, NB, W)
        for t in range(cx):
            acc = [jnp.zeros((NS, NB, LANES), _F32) for _ in range(B_HEADS)]
            for j in range(t + 1):
                p = q_sc[:, t] * k_sc[:, j]
                if j < t:
                    p = p * jnp.exp2(b_sc[:, t] - b_sc[:, j])
                vj = v4[:, j]
                for hh in range(B_HEADS):
                    sl = slice(hh * LANES, (hh + 1) * LANES)
                    acc[hh] = acc[hh] + jnp.sum(p[..., sl], axis=-1, keepdims=True) * vj[..., sl]
            od_sc[:, t] = jnp.concatenate(acc, axis=-1)

    def chunk_state():
        units = [(bi, hh) for bi in range(NB) for hh in range(B_HEADS)]
        for ui, (bi, hh) in enumerate(units):
            rows = pl.ds(bi, C, stride=NB)
            amat = jnp.zeros((C, C), _F32)
            for li in range(len(levels)):
                m = m_sc[li, hh, rows, :].astype(_BF)
                amat = amat + lax.dot_general(m, m, _NT, preferred_element_type=_F32) * mask_ref[li]
            am_sc[ui] = amat
        for ui, (bi, hh) in enumerate(units):
            rows = pl.ds(bi, C, stride=NB)
            qe = qe_sc[hh, rows, :].astype(_BF)
            v = v_sc[hh, rows, :].astype(_BF)
            o = lax.dot_general(qe, s_ref[bi, hh].astype(_BF), _NT, preferred_element_type=_F32)
            o_sc[hh, rows, :] = o + jnp.dot(am_sc[ui].astype(_BF), v, preferred_element_type=_F32)
        for ui, (bi, hh) in enumerate(units):
            rows = pl.ds(bi, C, stride=NB)
            kb = kb_sc[hh, rows, :].astype(_BF)
            v = v_sc[hh, rows, :].astype(_BF)
            upd = lax.dot_general(v, kb, _TN, preferred_element_type=_F32)
            s_ref[bi, hh] = s_ref[bi, hh] * eb_sc[hh, pl.ds(bi, 1), :] + upd

    def tiny_state():
        lane = lax.broadcasted_iota(jnp.int32, (LANES, NB), 1)

        def seq_body(j, carry):
            n = i * G + j
            rows = pl.ds(n, C, stride=NB)
            onehot = lane == n
            for hh in range(B_HEADS):
                st = s0_ref[j, hh]
                qe = _pad_rows(qe_sc[hh, rows, :], 2 * SUBLANES).astype(_BF)
                kb = _pad_rows(kb_sc[hh, rows, :], 2 * SUBLANES).astype(_BF)
                v = _pad_rows(v_sc[hh, rows, :], 2 * SUBLANES).astype(_BF)
                o = jnp.dot(qe, st.astype(_BF), preferred_element_type=_F32)
                o_sc[hh, rows, :] = o[:C]
                ebcol = jnp.sum(jnp.where(onehot, ebt_sc[hh], 0.0), axis=1, keepdims=True)
                upd = lax.dot_general(kb, v, _TN, preferred_element_type=_F32)
                s_ref[j, hh] = st * ebcol + upd
            return carry
        lax.fori_loop(0, G, seq_body, 0)

    def back():
        og = []
        od = od_sc[...].reshape(TB, W)
        for hh in range(B_HEADS):
            sl = slice(hh * LANES, (hh + 1) * LANES)
            o = o_sc[hh] + od[:, sl]
            gb = u_sc[:, 5 * W + hh * LANES:5 * W + (hh + 1) * LANES]
            ms = jnp.mean(o * o, axis=-1, keepdims=True)
            og.append(o * lax.rsqrt(ms + EPS) * gn_ref[...] * (gb * jax.nn.sigmoid(gb)))
        ya = hs_sc[...].reshape(TB, W) * jax.nn.gelu(u_sc[:, W:2 * W])
        y = jnp.concatenate([ya] + og, axis=-1).astype(_BF)
        xo_ref[...] = x_ref[...] + jnp.dot(y, wout_ref[...], preferred_element_type=_F32)

    if tiny:
        @pl.when(i == 0)
        def _():
            h_ref[...] = h0_ref[...]
            xp_sc[0:3] = c0_ref[...]
            front()

        tiny_state()

        @pl.when(i == pl.num_programs(0) - 1)
        def _():
            back()
    else:
        @pl.when(i == 0)
        def _():
            h_ref[...] = h0_ref[...]
            c_ref[...] = c0_ref[...]
            s_ref[...] = s0_ref[...]

        xp_sc[0:3] = c_ref[...]
        front()
        chunk_state()
        back()


def _const_spec(shape, single=True):
    nd = len(shape)
    if single:
        return pl.BlockSpec(shape, lambda i, _n=nd: (0,) * _n, pipeline_mode=pl.Buffered(1))
    return pl.BlockSpec(shape, lambda i, _n=nd: (0,) * _n)


def _mixer(x, h0, c0, s0, lw, *, NB, C, tiny):
    rows, D = x.shape
    W = D // 2
    TB = C * NB
    cx = min(EXACT, C)
    levels = _level_list(C)
    if tiny:
        G = min(SAMPLE_GROUP, NB)
        grid = (NB // G,)
        x_spec = _const_spec((TB, D))
        s_spec = pl.BlockSpec((G, B_HEADS, LANES, LANES), lambda i: (i, 0, 0, 0))
        assert rows == TB
    else:
        G = NB
        grid = (rows // TB,)
        x_spec = pl.BlockSpec((TB, D), lambda i: (i, 0))
        s_spec = _const_spec((NB, B_HEADS, LANES, LANES))
    masks = jnp.asarray(_level_masks(C))
    nl = max(len(levels), 1)
    weights = (lw["ln_mix"], lw["w_in"], lw["conv_w"], lw["conv_b"], lw["w_ra"], lw["b_ra"],
               lw["w_rx"], lw["b_rx"], lw["lam"], lw["lb"], lw["gn"], lw["w_out"], masks)
    in_specs = [x_spec, _const_spec(h0.shape), _const_spec(c0.shape), s_spec]
    in_specs += [_const_spec(w.shape) for w in weights]
    out_shape = (jax.ShapeDtypeStruct((rows, D), _F32), jax.ShapeDtypeStruct((NB, W), _F32),
                 jax.ShapeDtypeStruct((3, NB, W), _F32),
                 jax.ShapeDtypeStruct((NB, B_HEADS, LANES, LANES), _F32))
    xo_spec = _const_spec((TB, D), single=False) if tiny else x_spec
    so_spec = s_spec if tiny else _const_spec((NB, B_HEADS, LANES, LANES), single=False)
    out_specs = (xo_spec, _const_spec((NB, W), single=False), _const_spec((3, NB, W), single=False), so_spec)
    NS = C // cx
    scratch = [
        pltpu.VMEM((TB, 6 * W), _F32),
        pltpu.VMEM((C + 3, NB, W), _F32),
        pltpu.VMEM((C, NB, W), _F32),
        pltpu.VMEM((C, NB, W), _F32),
        pltpu.VMEM((NS, cx, NB, W), _F32),
        pltpu.VMEM((NS, cx, NB, W), _F32),
        pltpu.VMEM((NS, cx, NB, W), _F32),
        pltpu.VMEM((NS, cx, NB, W), _F32),
        pltpu.VMEM((B_HEADS, TB, LANES), _F32),
        pltpu.VMEM((B_HEADS, TB, LANES), _F32),
        pltpu.VMEM((B_HEADS, TB, LANES), _F32),
        pltpu.VMEM((nl, B_HEADS, TB, LANES), _F32),
        pltpu.VMEM((B_HEADS, TB, LANES), _F32),
        pltpu.VMEM((B_HEADS, NB, LANES), _F32),
        pltpu.VMEM((B_HEADS, LANES, NB if tiny else LANES), _F32),
        pltpu.VMEM((1 if tiny else NB * B_HEADS, C if not tiny else SUBLANES, C if not tiny else LANES), _F32),
    ]
    kern = functools.partial(_mixer_kernel, NB=NB, C=C, cx=cx, levels=levels, tiny=tiny, G=G)
    return pl.pallas_call(
        kern, grid=grid, in_specs=in_specs, out_specs=out_specs, out_shape=out_shape,
        scratch_shapes=scratch,
        compiler_params=pltpu.CompilerParams(dimension_semantics=("arbitrary",),
                                             vmem_limit_bytes=VMEM_LIMIT),
        name="mixer_tiny" if tiny else "mixer_chunk",
    )(x, h0, c0, s0, *weights)


def _mlp_kernel(x_ref, ln_ref, wup_ref, wdn_ref, lnf_ref, o_ref, *, final):
    x = x_ref[...]
    pn = _rms_rows(x, ln_ref[...]).astype(_BF)
    acc = x
    F = wup_ref.shape[1]
    for c0 in range(0, F, MLP_COLS):
        hmid = jnp.dot(pn, wup_ref[:, c0:c0 + MLP_COLS], preferred_element_type=_F32)
        hmid = jnp.square(jnp.maximum(hmid, 0.0)).astype(_BF)
        acc = acc + jnp.dot(hmid, wdn_ref[c0:c0 + MLP_COLS, :], preferred_element_type=_F32)
    if final:
        acc = _rms_rows(acc, lnf_ref[...])
    o_ref[...] = acc


def _mlp(x, lw, lnf, *, final):
    rows, D = x.shape
    TM = min(MLP_ROWS, rows)
    assert rows % TM == 0
    weights = (lw["ln_mlp"], lw["w_up"], lw["w_down"], lnf)
    x_spec = pl.BlockSpec((TM, D), lambda i: (i, 0))
    return pl.pallas_call(
        functools.partial(_mlp_kernel, final=final), grid=(rows // TM,),
        in_specs=[x_spec] + [_const_spec(w.shape) for w in weights],
        out_specs=x_spec, out_shape=jax.ShapeDtypeStruct((rows, D), _F32),
        compiler_params=pltpu.CompilerParams(dimension_semantics=("arbitrary",),
                                             vmem_limit_bytes=VMEM_LIMIT),
        name="mlp",
    )(x, *weights)


def _block_diag(w):
    H, I, J = w.shape
    eye = jnp.eye(H, dtype=w.dtype)
    return (w[:, :, None, :] * eye[:, None, :, None]).reshape(H * I, H * J)


def kernel(x_prompt, x_sample, state_rglru_h, state_rglru_conv, state_hgrn2, meta_tokens, ln_mix, w_in,
           conv_w, conv_b, w_rg_a, b_rg_a, w_rg_x, b_rg_x, rg_lambda, hgrn_lb, hgrn_norm, w_out, ln_mlp,
           w_up, w_down, ln_final):
    B, T, D = x_prompt.shape
    NS, TS, _ = x_sample.shape
    depth = w_in.shape[0]
    W = D // 2
    assert B == SUBLANES and W == B_HEADS * LANES and T % PROMPT_CHUNK == 0

    lbs = jax.nn.softmax(hgrn_lb.astype(_F32), axis=0)
    lb_all = jnp.clip(jnp.cumsum(lbs, axis=0) - lbs[0], 0.0, 1.0)

    xp = jnp.swapaxes(x_prompt, 0, 1).reshape(T * B, D)
    xs = jnp.swapaxes(x_sample, 0, 1).reshape(TS * NS, D)
    xm = jnp.broadcast_to(meta_tokens.astype(_F32)[:, None, :], (N_META, B, D)).reshape(N_META * B, D)
    conv_s = jnp.swapaxes(state_rglru_conv, 1, 2)
    lnf = ln_final.reshape(1, D)

    zero_h = jnp.zeros((B, W), _F32)
    zero_c = jnp.zeros((CONV_W - 1, B, W), _F32)
    zero_s = jnp.zeros((B, B_HEADS, LANES, LANES), _F32)

    outs = {k: [] for k in ("ph", "pc", "pS", "sh", "sc", "sS")}
    for l in range(depth):
        lw = dict(
            ln_mix=ln_mix[l].reshape(1, D), w_in=w_in[l].astype(_BF), conv_w=conv_w[l],
            conv_b=conv_b[l].reshape(1, W), w_ra=_block_diag(w_rg_a[l]).astype(_BF),
            b_ra=b_rg_a[l].reshape(1, W), w_rx=_block_diag(w_rg_x[l]).astype(_BF),
            b_rx=b_rg_x[l].reshape(1, W), lam=rg_lambda[l].reshape(1, W), lb=lb_all[l].reshape(1, W),
            gn=hgrn_norm[l].reshape(1, LANES), w_out=w_out[l].astype(_BF),
            ln_mlp=ln_mlp[l].reshape(1, D), w_up=w_up[l].astype(_BF), w_down=w_down[l].astype(_BF))
        final = l == depth - 1
        xm, hm, cm, sm = _mixer(xm, zero_h, zero_c, zero_s, lw, NB=B, C=N_META, tiny=False)
        if not final:
            xm = _mlp(xm, lw, lnf, final=False)
        xp, hp, cp, sp = _mixer(xp, hm, cm, sm, lw, NB=B, C=PROMPT_CHUNK, tiny=False)
        xp = _mlp(xp, lw, lnf, final=final)
        xs, hs, cs, ss = _mixer(xs, state_rglru_h[l], conv_s[l], state_hgrn2[l], lw, NB=NS, C=TS, tiny=True)
        xs = _mlp(xs, lw, lnf, final=final)
        outs["ph"].append(hp)
        outs["pc"].append(jnp.swapaxes(cp, 0, 1))
        outs["pS"].append(jnp.swapaxes(sp, -1, -2))
        outs["sh"].append(hs)
        outs["sc"].append(jnp.swapaxes(cs, 0, 1))
        outs["sS"].append(ss)

    y_prompt = jnp.swapaxes(xp.reshape(T, B, D), 0, 1)
    y_sample = jnp.swapaxes(xs.reshape(TS, NS, D), 0, 1)
    return (y_prompt, y_sample, jnp.stack(outs["ph"]), jnp.stack(outs["pc"]), jnp.stack(outs["pS"]),
            jnp.stack(outs["sh"]), jnp.stack(outs["sc"]), jnp.stack(outs["sS"]))
```

```python
import functools

import jax
import jax.numpy as jnp
import numpy as np
from jax import lax
from jax.experimental import pallas as pl
from jax.experimental.pallas import tpu as pltpu

N_META = 16
CONV_W = 4
A_HEADS = 8
B_HEADS = 4
RG_C = 8.0
EPS = 1e-6
F_MIN = 1e-30
LOG2E = 1.4426950408889634

LANES = 128
SUBLANES = 8
EXACT = 8
PROMPT_CHUNK = 64
SAMPLE_GROUP = 8
MLP_ROWS = 512
MLP_COLS = 1024
VMEM_LIMIT = 56 * 1024 * 1024

_BF = jnp.bfloat16
_F32 = jnp.float32
_NT = (((1,), (1,)), ((), ()))
_TN = (((0,), (0,)), ((), ()))


def _rms_rows(x, g):
    ms = jnp.mean(x * x, axis=-1, keepdims=True)
    return x * lax.rsqrt(ms + EPS) * g


def _sigmoid_pair(x):
    e = jnp.exp(-jnp.abs(x))
    inv = 1.0 / (1.0 + e)
    big, small = inv, e * inv
    pos = x >= 0
    return jnp.where(pos, big, small), jnp.where(pos, small, big)


def _pad_rows(x, rows):
    if x.shape[0] == rows:
        return x
    return jnp.concatenate([x, jnp.zeros((rows - x.shape[0], x.shape[1]), x.dtype)], axis=0)


def _level_list(chunk):
    out, lv = [], EXACT
    while 2 * lv <= chunk:
        out.append(lv)
        lv *= 2
    return tuple(out)


def _level_masks(chunk):
    lv = _level_list(chunk)
    t = np.arange(chunk)[:, None]
    s = np.arange(chunk)[None, :]
    ms = [((t // (2 * L) == s // (2 * L)) & (t % (2 * L) >= L) & (s % (2 * L) < L)) for L in lv]
    if not ms:
        return np.zeros((1, SUBLANES, LANES), np.float32)
    return np.stack(ms).astype(np.float32)


def _mixer_kernel(x_ref, h0_ref, c0_ref, s0_ref, lnm_ref, win_ref, cw_ref, cb_ref, wra_ref,
                  bra_ref, wrx_ref, brx_ref, lam_ref, lb_ref, gn_ref, wout_ref, mask_ref,
                  xo_ref, h_ref, c_ref, s_ref,
                  u_sc, xp_sc, a_sc, hs_sc, q_sc, k_sc, b_sc, od_sc,
                  qe_sc, kb_sc, v_sc, m_sc, o_sc, eb_sc, ebt_sc, am_sc,
                  *, NB, C, cx, levels, tiny, G):
    W = h_ref.shape[-1]
    TB = C * NB
    NS = C // cx
    i = pl.program_id(0)

    def front():
        hn = _rms_rows(x_ref[...], lnm_ref[...]).astype(_BF)

        def proj(g):
            u_sc[:, g * W:(g + 1) * W] = jnp.dot(hn, win_ref[:, g * W:(g + 1) * W],
                                                 preferred_element_type=_F32)

        proj(0)
        proj(3)
        proj(2)
        xp_sc[3:] = u_sc[:, 0:W].reshape(C, NB, W)
        cw = cw_ref[...]
        xc = (cb_ref[...] + xp_sc[3:C + 3] * cw[3:4] + xp_sc[2:C + 2] * cw[2:3]
              + xp_sc[1:C + 1] * cw[1:2] + xp_sc[0:C] * cw[0:1])
        c_ref[...] = xp_sc[C:C + 3]
        xc2 = xc.reshape(TB, W)
        xcb = xc2.astype(_BF)
        r = jax.nn.sigmoid(jnp.dot(xcb, wra_ref[...], preferred_element_type=_F32) + bra_ref[...])
        ig = jax.nn.sigmoid(jnp.dot(xcb, wrx_ref[...], preferred_element_type=_F32) + brx_ref[...])
        proj(4)
        proj(5)
        proj(1)
        lam = lam_ref[...]
        sp = jnp.maximum(-lam, 0.0) + jnp.log1p(jnp.exp(-jnp.abs(lam)))
        a = jnp.exp((-RG_C) * r * sp)
        om = 1.0 - a * a
        bt = jnp.where(om > 0.0, om * lax.rsqrt(om), 0.0) * ig * xc2
        a_sc[...] = a.reshape(C, NB, W)
        hs_sc[...] = bt.reshape(C, NB, W)
        h = h_ref[...]
        for t in range(C):
            h = a_sc[t] * h + hs_sc[t]
            hs_sc[t] = h
        h_ref[...] = h

        lb = lb_ref[...]
        qb = u_sc[:, 2 * W:3 * W]
        q = qb * jax.nn.sigmoid(qb) * (LANES ** -0.5)
        sg, sgn = _sigmoid_pair(u_sc[:, 3 * W:4 * W])
        f = lb + (1.0 - lb) * sg
        k = (1.0 - lb) * sgn
        logf = jnp.log(jnp.maximum(f, F_MIN)) * LOG2E
        q_sc[...] = q.reshape(NS, cx, NB, W)
        k_sc[...] = k.reshape(NS, cx, NB, W)
        b_sc[...] = logf.reshape(NS, cx, NB, W)
        b = jnp.zeros((NB, W), _F32)
        for t in range(C):
            b = b + b_sc[t // cx, t % cx]
            b_sc[t // cx, t % cx] = b
        b_last = b
        vv = u_sc[:, 4 * W:5 * W]
        b2 = b_sc[...].reshape(TB, W)
        q2 = q_sc[...].reshape(TB, W)
        k2 = k_sc[...].reshape(TB, W)
        qe = q2 * jnp.exp2(b2)
        kb = (k2.reshape(C, NB, W) * jnp.exp2(b_last[None] - b2.reshape(C, NB, W))).reshape(TB, W)
        eb = jnp.exp2(b_last)
        for hh in range(B_HEADS):
            sl = slice(hh * LANES, (hh + 1) * LANES)
            qe_sc[hh] = qe[:, sl]
            kb_sc[hh] = kb[:, sl]
            v_sc[hh] = vv[:, sl]
            eb_sc[hh] = eb[:, sl]
            if tiny:
                ebt_sc[hh] = eb[:, sl].T

        for li, L in enumerate(levels):
            spb = (2 * L) // cx
            for blk in range(C // (2 * L)):
                mid = blk * 2 * L + L - 1
                rmid = b_sc[mid // cx, mid % cx][None]
                bb = b_sc[blk * spb:(blk + 1) * spb].reshape(2 * L, NB, W)
                kk = k_sc[blk * spb:(blk + 1) * spb].reshape(2 * L, NB, W)
                qq = q_sc[blk * spb:(blk + 1) * spb].reshape(2 * L, NB, W)
                lo = (kk[:L] * jnp.exp2(rmid - bb[:L])).reshape(L * NB, W)
                hi = (qq[L:] * jnp.exp2(bb[L:] - rmid)).reshape(L * NB, W)
                r0 = blk * 2 * L * NB
                for hh in range(B_HEADS):
                    sl = slice(hh * LANES, (hh + 1) * LANES)
                    m_sc[li, hh, r0:r0 + L * NB, :] = lo[:, sl]
                    m_sc[li, hh, r0 + L * NB:r0 + 2 * L * NB, :] = hi[:, sl]

        v4 = vv.reshape(NS, cx, NB, W)
        for t in range(cx):
            acc = [jnp.zeros((NS, NB, LANES), _F32) for _ in range(B_HEADS)]
            for j in range(t + 1):
                p = q_sc[:, t] * k_sc[:, j]
                if j < t:
                    p = p * jnp.exp2(b_sc[:, t] - b_sc[:, j])
                vj = v4[:, j]
                for hh in range(B_HEADS):
                    sl = slice(hh * LANES, (hh + 1) * LANES)
                    acc[hh] = acc[hh] + jnp.sum(p[..., sl], axis=-1, keepdims=True) * vj[..., sl]
            od_sc[:, t] = jnp.concatenate(acc, axis=-1)

    def chunk_state():
        units = [(bi, hh) for bi in range(NB) for hh in range(B_HEADS)]
        for ui, (bi, hh) in enumerate(units):
            rows = pl.ds(bi, C, stride=NB)
            amat = jnp.zeros((C, C), _F32)
            for li in range(len(levels)):
                m = m_sc[li, hh, rows, :].astype(_BF)
                amat = amat + lax.dot_general(m, m, _NT, preferred_element_type=_F32) * mask_ref[li]
            am_sc[ui] = amat
        for ui, (bi, hh) in enumerate(units):
            rows = pl.ds(bi, C, stride=NB)
            qe = qe_sc[hh, rows, :].astype(_BF)
            v = v_sc[hh, rows, :].astype(_BF)
            o = lax.dot_general(qe, s_ref[bi, hh].astype(_BF), _NT, preferred_element_type=_F32)
            o_sc[hh, rows, :] = o + jnp.dot(am_sc[ui].astype(_BF), v, preferred_element_type=_F32)
        for ui, (bi, hh) in enumerate(units):
            rows = pl.ds(bi, C, stride=NB)
            kb = kb_sc[hh, rows, :].astype(_BF)
            v = v_sc[hh, rows, :].astype(_BF)
            upd = lax.dot_general(v, kb, _TN, preferred_element_type=_F32)
            s_ref[bi, hh] = s_ref[bi, hh] * eb_sc[hh, pl.ds(bi, 1), :] + upd

    def tiny_state():
        lane = lax.broadcasted_iota(jnp.int32, (LANES, NB), 1)

        units = [(j, hh) for j in range(G) for hh in range(B_HEADS)]
        for j, hh in units:
            rows = pl.ds(i * G + j, C, stride=NB)
            qe = _pad_rows(qe_sc[hh, rows, :], 2 * SUBLANES).astype(_BF)
            o = jnp.dot(qe, s0_ref[j, hh].astype(_BF), preferred_element_type=_F32)
            o_sc[hh, rows, :] = o[:C]
        for j, hh in units:
            n = i * G + j
            rows = pl.ds(n, C, stride=NB)
            kb = _pad_rows(kb_sc[hh, rows, :], 2 * SUBLANES).astype(_BF)
            v = _pad_rows(v_sc[hh, rows, :], 2 * SUBLANES).astype(_BF)
            ebcol = jnp.sum(jnp.where(lane == n, ebt_sc[hh], 0.0), axis=1, keepdims=True)
            upd = lax.dot_general(kb, v, _TN, preferred_element_type=_F32)
            s_ref[j, hh] = s0_ref[j, hh] * ebcol + upd

    def back():
        og = []
        od = od_sc[...].reshape(TB, W)
        for hh in range(B_HEADS):
            sl = slice(hh * LANES, (hh + 1) * LANES)
            o = o_sc[hh] + od[:, sl]
            gb = u_sc[:, 5 * W + hh * LANES:5 * W + (hh + 1) * LANES]
            ms = jnp.mean(o * o, axis=-1, keepdims=True)
            og.append(o * lax.rsqrt(ms + EPS) * gn_ref[...] * (gb * jax.nn.sigmoid(gb)))
        ya = hs_sc[...].reshape(TB, W) * jax.nn.gelu(u_sc[:, W:2 * W])
        y = jnp.concatenate([ya] + og, axis=-1).astype(_BF)
        xo_ref[...] = x_ref[...] + jnp.dot(y, wout_ref[...], preferred_element_type=_F32)

    if tiny:
        @pl.when(i == 0)
        def _():
            h_ref[...] = h0_ref[...]
            xp_sc[0:3] = c0_ref[...]
            front()

        tiny_state()

        @pl.when(i == pl.num_programs(0) - 1)
        def _():
            back()
    else:
        @pl.when(i == 0)
        def _():
            h_ref[...] = h0_ref[...]
            c_ref[...] = c0_ref[...]
            s_ref[...] = s0_ref[...]

        xp_sc[0:3] = c_ref[...]
        front()
        chunk_state()
        back()


def _const_spec(shape, single=True):
    nd = len(shape)
    if single:
        return pl.BlockSpec(shape, lambda i, _n=nd: (0,) * _n, pipeline_mode=pl.Buffered(1))
    return pl.BlockSpec(shape, lambda i, _n=nd: (0,) * _n)


def _mixer(x, h0, c0, s0, lw, *, NB, C, tiny, layer=0):
    rows, D = x.shape
    W = D // 2
    TB = C * NB
    cx = min(EXACT, C)
    levels = _level_list(C)
    if tiny:
        G = min(SAMPLE_GROUP, NB)
        grid = (NB // G,)
        x_spec = _const_spec((TB, D))
        s_spec = pl.BlockSpec((G, B_HEADS, LANES, LANES), lambda i: (i, 0, 0, 0))
        s0_spec = pl.BlockSpec((None, G, B_HEADS, LANES, LANES), lambda i: (layer, i, 0, 0, 0))
        assert rows == TB
    else:
        G = NB
        grid = (rows // TB,)
        x_spec = pl.BlockSpec((TB, D), lambda i: (i, 0))
        s_spec = s0_spec = _const_spec((NB, B_HEADS, LANES, LANES))
    masks = jnp.asarray(_level_masks(C))
    nl = max(len(levels), 1)
    weights = (lw["ln_mix"], lw["w_in"], lw["conv_w"], lw["conv_b"], lw["w_ra"], lw["b_ra"],
               lw["w_rx"], lw["b_rx"], lw["lam"], lw["lb"], lw["gn"], lw["w_out"], masks)
    in_specs = [x_spec, _const_spec(h0.shape), _const_spec(c0.shape), s0_spec]
    in_specs += [_const_spec(w.shape) for w in weights]
    out_shape = (jax.ShapeDtypeStruct((rows, D), _F32), jax.ShapeDtypeStruct((NB, W), _F32),
                 jax.ShapeDtypeStruct((3, NB, W), _F32),
                 jax.ShapeDtypeStruct((NB, B_HEADS, LANES, LANES), _F32))
    xo_spec = _const_spec((TB, D), single=False) if tiny else x_spec
    so_spec = s_spec if tiny else _const_spec((NB, B_HEADS, LANES, LANES), single=False)
    out_specs = (xo_spec, _const_spec((NB, W), single=False), _const_spec((3, NB, W), single=False), so_spec)
    NS = C // cx
    scratch = [
        pltpu.VMEM((TB, 6 * W), _F32),
        pltpu.VMEM((C + 3, NB, W), _F32),
        pltpu.VMEM((C, NB, W), _F32),
        pltpu.VMEM((C, NB, W), _F32),
        pltpu.VMEM((NS, cx, NB, W), _F32),
        pltpu.VMEM((NS, cx, NB, W), _F32),
        pltpu.VMEM((NS, cx, NB, W), _F32),
        pltpu.VMEM((NS, cx, NB, W), _F32),
        pltpu.VMEM((B_HEADS, TB, LANES), _F32),
        pltpu.VMEM((B_HEADS, TB, LANES), _F32),
        pltpu.VMEM((B_HEADS, TB, LANES), _F32),
        pltpu.VMEM((nl, B_HEADS, TB, LANES), _F32),
        pltpu.VMEM((B_HEADS, TB, LANES), _F32),
        pltpu.VMEM((B_HEADS, NB, LANES), _F32),
        pltpu.VMEM((B_HEADS, LANES, NB if tiny else LANES), _F32),
        pltpu.VMEM((1 if tiny else NB * B_HEADS, C if not tiny else SUBLANES, C if not tiny else LANES), _F32),
    ]
    kern = functools.partial(_mixer_kernel, NB=NB, C=C, cx=cx, levels=levels, tiny=tiny, G=G)
    return pl.pallas_call(
        kern, grid=grid, in_specs=in_specs, out_specs=out_specs, out_shape=out_shape,
        scratch_shapes=scratch,
        compiler_params=pltpu.CompilerParams(dimension_semantics=("arbitrary",),
                                             vmem_limit_bytes=VMEM_LIMIT),
        name="mixer_tiny" if tiny else "mixer_chunk",
    )(x, h0, c0, s0, *weights)


def _mlp_kernel(x_ref, ln_ref, wup_ref, wdn_ref, lnf_ref, o_ref, *, final):
    x = x_ref[...]
    pn = _rms_rows(x, ln_ref[...]).astype(_BF)
    acc = x
    F = wup_ref.shape[1]
    for c0 in range(0, F, MLP_COLS):
        hmid = jnp.dot(pn, wup_ref[:, c0:c0 + MLP_COLS], preferred_element_type=_F32)
        hmid = jnp.square(jnp.maximum(hmid, 0.0)).astype(_BF)
        acc = acc + jnp.dot(hmid, wdn_ref[c0:c0 + MLP_COLS, :], preferred_element_type=_F32)
    if final:
        acc = _rms_rows(acc, lnf_ref[...])
    o_ref[...] = acc


def _mlp(x, lw, lnf, *, final):
    rows, D = x.shape
    TM = min(MLP_ROWS, rows)
    assert rows % TM == 0
    weights = (lw["ln_mlp"], lw["w_up"], lw["w_down"], lnf)
    x_spec = pl.BlockSpec((TM, D), lambda i: (i, 0))
    return pl.pallas_call(
        functools.partial(_mlp_kernel, final=final), grid=(rows // TM,),
        in_specs=[x_spec] + [_const_spec(w.shape) for w in weights],
        out_specs=x_spec, out_shape=jax.ShapeDtypeStruct((rows, D), _F32),
        compiler_params=pltpu.CompilerParams(dimension_semantics=("arbitrary",),
                                             vmem_limit_bytes=VMEM_LIMIT),
        name="mlp",
    )(x, *weights)


def _block_diag(w):
    H, I, J = w.shape
    eye = jnp.eye(H, dtype=w.dtype)
    return (w[:, :, None, :] * eye[:, None, :, None]).reshape(H * I, H * J)


def kernel(x_prompt, x_sample, state_rglru_h, state_rglru_conv, state_hgrn2, meta_tokens, ln_mix, w_in,
           conv_w, conv_b, w_rg_a, b_rg_a, w_rg_x, b_rg_x, rg_lambda, hgrn_lb, hgrn_norm, w_out, ln_mlp,
           w_up, w_down, ln_final):
    B, T, D = x_prompt.shape
    NS, TS, _ = x_sample.shape
    depth = w_in.shape[0]
    W = D // 2
    assert B == SUBLANES and W == B_HEADS * LANES and T % PROMPT_CHUNK == 0

    lbs = jax.nn.softmax(hgrn_lb.astype(_F32), axis=0)
    lb_all = jnp.clip(jnp.cumsum(lbs, axis=0) - lbs[0], 0.0, 1.0)

    xp = jnp.swapaxes(x_prompt, 0, 1).reshape(T * B, D)
    xs = jnp.swapaxes(x_sample, 0, 1).reshape(TS * NS, D)
    xm = jnp.broadcast_to(meta_tokens.astype(_F32)[:, None, :], (N_META, B, D)).reshape(N_META * B, D)
    conv_s = jnp.swapaxes(state_rglru_conv, 1, 2)
    lnf = ln_final.reshape(1, D)

    zero_h = jnp.zeros((B, W), _F32)
    zero_c = jnp.zeros((CONV_W - 1, B, W), _F32)
    zero_s = jnp.zeros((B, B_HEADS, LANES, LANES), _F32)

    outs = {k: [] for k in ("ph", "pc", "pS", "sh", "sc", "sS")}
    for l in range(depth):
        lw = dict(
            ln_mix=ln_mix[l].reshape(1, D), w_in=w_in[l].astype(_BF), conv_w=conv_w[l],
            conv_b=conv_b[l].reshape(1, W), w_ra=_block_diag(w_rg_a[l]).astype(_BF),
            b_ra=b_rg_a[l].reshape(1, W), w_rx=_block_diag(w_rg_x[l]).astype(_BF),
            b_rx=b_rg_x[l].reshape(1, W), lam=rg_lambda[l].reshape(1, W), lb=lb_all[l].reshape(1, W),
            gn=hgrn_norm[l].reshape(1, LANES), w_out=w_out[l].astype(_BF),
            ln_mlp=ln_mlp[l].reshape(1, D), w_up=w_up[l].astype(_BF), w_down=w_down[l].astype(_BF))
        final = l == depth - 1
        xm, hm, cm, sm = _mixer(xm, zero_h, zero_c, zero_s, lw, NB=B, C=N_META, tiny=False)
        if not final:
            xm = _mlp(xm, lw, lnf, final=False)
        xp, hp, cp, sp = _mixer(xp, hm, cm, sm, lw, NB=B, C=PROMPT_CHUNK, tiny=False)
        xp = _mlp(xp, lw, lnf, final=final)
        xs, hs, cs, ss = _mixer(xs, state_rglru_h[l], conv_s[l], state_hgrn2, lw, NB=NS, C=TS, tiny=True,
                                layer=l)
        xs = _mlp(xs, lw, lnf, final=final)
        outs["ph"].append(hp)
        outs["pc"].append(jnp.swapaxes(cp, 0, 1))
        outs["pS"].append(jnp.swapaxes(sp, -1, -2))
        outs["sh"].append(hs)
        outs["sc"].append(jnp.swapaxes(cs, 0, 1))
        outs["sS"].append(ss)

    y_prompt = jnp.swapaxes(xp.reshape(T, B, D), 0, 1)
    y_sample = jnp.swapaxes(xs.reshape(TS, NS, D), 0, 1)
    return (y_prompt, y_sample, jnp.stack(outs["ph"]), jnp.stack(outs["pc"]), jnp.stack(outs["pS"]),
            jnp.stack(outs["sh"]), jnp.stack(outs["sc"]), jnp.stack(outs["sS"]))
```

```python
import functools

import jax
import jax.numpy as jnp
import numpy as np
from jax import lax
from jax.experimental import pallas as pl
from jax.experimental.pallas import tpu as pltpu

N_META = 16
CONV_W = 4
A_HEADS = 8
B_HEADS = 4
RG_C = 8.0
EPS = 1e-6
F_MIN = 1e-30
LOG2E = 1.4426950408889634

LANES = 128
SUBLANES = 8
EXACT = 8
PROMPT_CHUNK = 64
SAMPLE_GROUP = 8
MLP_ROWS = 512
MLP_COLS = 1024
VMEM_LIMIT = 56 * 1024 * 1024

_BF = jnp.bfloat16
_F32 = jnp.float32
_NT = (((1,), (1,)), ((), ()))
_TN = (((0,), (0,)), ((), ()))


def _rms_rows(x, g):
    ms = jnp.mean(x * x, axis=-1, keepdims=True)
    return x * lax.rsqrt(ms + EPS) * g


def _sigmoid_pair(x):
    e = jnp.exp(-jnp.abs(x))
    inv = 1.0 / (1.0 + e)
    big, small = inv, e * inv
    pos = x >= 0
    return jnp.where(pos, big, small), jnp.where(pos, small, big)


def _pad_rows(x, rows):
    if x.shape[0] == rows:
        return x
    return jnp.concatenate([x, jnp.zeros((rows - x.shape[0], x.shape[1]), x.dtype)], axis=0)


def _level_list(chunk):
    out, lv = [], EXACT
    while 2 * lv <= chunk:
        out.append(lv)
        lv *= 2
    return tuple(out)


def _level_masks(chunk):
    lv = _level_list(chunk)
    t = np.arange(chunk)[:, None]
    s = np.arange(chunk)[None, :]
    ms = [((t // (2 * L) == s // (2 * L)) & (t % (2 * L) >= L) & (s % (2 * L) < L)) for L in lv]
    if not ms:
        return np.zeros((1, SUBLANES, LANES), np.float32)
    return np.stack(ms).astype(np.float32)


def _mixer_kernel(x_ref, h0_ref, c0_ref, s0_ref, lnm_ref, win_ref, cw_ref, cb_ref, wra_ref,
                  bra_ref, wrx_ref, brx_ref, lam_ref, lb_ref, gn_ref, wout_ref, mask_ref,
                  xo_ref, h_ref, c_ref, s_ref,
                  u_sc, xp_sc, a_sc, hs_sc, q_sc, k_sc, b_sc, od_sc,
                  qe_sc, kb_sc, v_sc, m_sc, o_sc, eb_sc, ebt_sc, am_sc,
                  *, NB, C, cx, levels, tiny, G):
    W = h_ref.shape[-1]
    TB = C * NB
    NS = C // cx
    i = pl.program_id(0)

    def front():
        hn = _rms_rows(x_ref[...], lnm_ref[...]).astype(_BF)

        def proj(g):
            u_sc[:, g * W:(g + 1) * W] = jnp.dot(hn, win_ref[:, g * W:(g + 1) * W],
                                                 preferred_element_type=_F32)

        proj(0)
        proj(3)
        proj(2)
        xp_sc[3:] = u_sc[:, 0:W].reshape(C, NB, W)
        cw = cw_ref[...]
        xc = (cb_ref[...] + xp_sc[3:C + 3] * cw[3:4] + xp_sc[2:C + 2] * cw[2:3]
              + xp_sc[1:C + 1] * cw[1:2] + xp_sc[0:C] * cw[0:1])
        c_ref[...] = xp_sc[C:C + 3]
        xc2 = xc.reshape(TB, W)
        xcb = xc2.astype(_BF)
        r = jax.nn.sigmoid(jnp.dot(xcb, wra_ref[...], preferred_element_type=_F32) + bra_ref[...])
        ig = jax.nn.sigmoid(jnp.dot(xcb, wrx_ref[...], preferred_element_type=_F32) + brx_ref[...])
        proj(4)
        proj(5)
        proj(1)
        lam = lam_ref[...]
        sp = jnp.maximum(-lam, 0.0) + jnp.log1p(jnp.exp(-jnp.abs(lam)))
        a = jnp.exp((-RG_C) * r * sp)
        om = 1.0 - a * a
        bt = jnp.where(om > 0.0, om * lax.rsqrt(om), 0.0) * ig * xc2
        a_sc[...] = a.reshape(C, NB, W)
        hs_sc[...] = bt.reshape(C, NB, W)
        h = h_ref[...]
        for t in range(C):
            h = a_sc[t] * h + hs_sc[t]
            hs_sc[t] = h
        h_ref[...] = h

        lb = lb_ref[...]
        qb = u_sc[:, 2 * W:3 * W]
        q = qb * jax.nn.sigmoid(qb) * (LANES ** -0.5)
        sg, sgn = _sigmoid_pair(u_sc[:, 3 * W:4 * W])
        f = lb + (1.0 - lb) * sg
        k = (1.0 - lb) * sgn
        logf = jnp.log(jnp.maximum(f, F_MIN)) * LOG2E
        q_sc[...] = q.reshape(NS, cx, NB, W)
        k_sc[...] = k.reshape(NS, cx, NB, W)
        b_sc[...] = logf.reshape(NS, cx, NB, W)
        b = jnp.zeros((NB, W), _F32)
        for t in range(C):
            b = b + b_sc[t // cx, t % cx]
            b_sc[t // cx, t % cx] = b
        b_last = b
        vv = u_sc[:, 4 * W:5 * W]
        b2 = b_sc[...].reshape(TB, W)
        q2 = q_sc[...].reshape(TB, W)
        k2 = k_sc[...].reshape(TB, W)
        qe = q2 * jnp.exp2(b2)
        kb = (k2.reshape(C, NB, W) * jnp.exp2(b_last[None] - b2.reshape(C, NB, W))).reshape(TB, W)
        eb = jnp.exp2(b_last)
        for hh in range(B_HEADS):
            sl = slice(hh * LANES, (hh + 1) * LANES)
            qe_sc[hh] = qe[:, sl]
            kb_sc[hh] = kb[:, sl]
            v_sc[hh] = vv[:, sl]
            eb_sc[hh] = eb[:, sl]
            if tiny:
                ebt_sc[hh] = eb[:, sl].T

        for li, L in enumerate(levels):
            spb = (2 * L) // cx
            for blk in range(C // (2 * L)):
                mid = blk * 2 * L + L - 1
                rmid = b_sc[mid // cx, mid % cx][None]
                bb = b_sc[blk * spb:(blk + 1) * spb].reshape(2 * L, NB, W)
                kk = k_sc[blk * spb:(blk + 1) * spb].reshape(2 * L, NB, W)
                qq = q_sc[blk * spb:(blk + 1) * spb].reshape(2 * L, NB, W)
                lo = (kk[:L] * jnp.exp2(rmid - bb[:L])).reshape(L * NB, W)
                hi = (qq[L:] * jnp.exp2(bb[L:] - rmid)).reshape(L * NB, W)
                r0 = blk * 2 * L * NB
                for hh in range(B_HEADS):
                    sl = slice(hh * LANES, (hh + 1) * LANES)
                    m_sc[li, hh, r0:r0 + L * NB, :] = lo[:, sl]
                    m_sc[li, hh, r0 + L * NB:r0 + 2 * L * NB, :] = hi[:, sl]

        v4 = vv.reshape(NS, cx, NB, W)
        for t in range(cx):
            acc = [jnp.zeros((NS, NB, LANES), _F32) for _ in range(B_HEADS)]
            for j in range(t + 1):
                p = q_sc[:, t] * k_sc[:, j]
                if j < t:
                    p = p * jnp.exp2(b_sc[:, t] - b_sc[:, j])
                vj = v4[:, j]
                for hh in range(B_HEADS):
                    sl = slice(hh * LANES, (hh + 1) * LANES)
                    acc[hh] = acc[hh] + jnp.sum(p[..., sl], axis=-1, keepdims=True) * vj[..., sl]
            od_sc[:, t] = jnp.concatenate(acc, axis=-1)

    def chunk_state():
        units = [(bi, hh) for bi in range(NB) for hh in range(B_HEADS)]
        for ui, (bi, hh) in enumerate(units):
            rows = pl.ds(bi, C, stride=NB)
            amat = jnp.zeros((C, C), _F32)
            for li in range(len(levels)):
                m = m_sc[li, hh, rows, :].astype(_BF)
                amat = amat + lax.dot_general(m, m, _NT, preferred_element_type=_F32) * mask_ref[li]
            am_sc[ui] = amat
        for ui, (bi, hh) in enumerate(units):
            rows = pl.ds(bi, C, stride=NB)
            qe = qe_sc[hh, rows, :].astype(_BF)
            v = v_sc[hh, rows, :].astype(_BF)
            o = lax.dot_general(qe, s_ref[bi, hh].astype(_BF), _NT, preferred_element_type=_F32)
            o_sc[hh, rows, :] = o + jnp.dot(am_sc[ui].astype(_BF), v, preferred_element_type=_F32)
        for ui, (bi, hh) in enumerate(units):
            rows = pl.ds(bi, C, stride=NB)
            kb = kb_sc[hh, rows, :].astype(_BF)
            v = v_sc[hh, rows, :].astype(_BF)
            upd = lax.dot_general(v, kb, _TN, preferred_element_type=_F32)
            s_ref[bi, hh] = s_ref[bi, hh] * eb_sc[hh, pl.ds(bi, 1), :] + upd

    def tiny_state():
        lane = lax.broadcasted_iota(jnp.int32, (LANES, NB), 1)

        units = [(j, hh) for j in range(G) for hh in range(B_HEADS)]
        for j, hh in units:
            rows = pl.ds(i * G + j, C, stride=NB)
            qe = _pad_rows(qe_sc[hh, rows, :], 2 * SUBLANES).astype(_BF)
            o = jnp.dot(qe, s0_ref[j, hh].astype(_BF), preferred_element_type=_F32)
            o_sc[hh, rows, :] = o[:C]
        for j, hh in units:
            n = i * G + j
            rows = pl.ds(n, C, stride=NB)
            kb = _pad_rows(kb_sc[hh, rows, :], 2 * SUBLANES).astype(_BF)
            v = _pad_rows(v_sc[hh, rows, :], 2 * SUBLANES).astype(_BF)
            ebcol = jnp.sum(jnp.where(lane == n, ebt_sc[hh], 0.0), axis=1, keepdims=True)
            upd = lax.dot_general(kb, v, _TN, preferred_element_type=_F32)
            s_ref[j, hh] = s0_ref[j, hh] * ebcol + upd

    def back():
        og = []
        od = od_sc[...].reshape(TB, W)
        for hh in range(B_HEADS):
            sl = slice(hh * LANES, (hh + 1) * LANES)
            o = o_sc[hh] + od[:, sl]
            gb = u_sc[:, 5 * W + hh * LANES:5 * W + (hh + 1) * LANES]
            ms = jnp.mean(o * o, axis=-1, keepdims=True)
            og.append(o * lax.rsqrt(ms + EPS) * gn_ref[...] * (gb * jax.nn.sigmoid(gb)))
        ya = hs_sc[...].reshape(TB, W) * jax.nn.gelu(u_sc[:, W:2 * W])
        y = jnp.concatenate([ya] + og, axis=-1).astype(_BF)
        xo_ref[...] = x_ref[...] + jnp.dot(y, wout_ref[...], preferred_element_type=_F32)

    if tiny:
        @pl.when(i == 0)
        def _():
            h_ref[...] = h0_ref[...]
            xp_sc[0:3] = c0_ref[...]
            front()

        tiny_state()

        @pl.when(i == pl.num_programs(0) - 1)
        def _():
            back()
    else:
        @pl.when(i == 0)
        def _():
            h_ref[...] = h0_ref[...]
            c_ref[...] = c0_ref[...]
            s_ref[...] = s0_ref[...]

        xp_sc[0:3] = c_ref[...]
        front()
        chunk_state()
        back()


def _const_spec(shape, single=True):
    nd = len(shape)
    if single:
        return pl.BlockSpec(shape, lambda i, _n=nd: (0,) * _n, pipeline_mode=pl.Buffered(1))
    return pl.BlockSpec(shape, lambda i, _n=nd: (0,) * _n)


def _layer_spec(arr, layer):
    return pl.BlockSpec((None,) + arr.shape[1:], lambda i: (layer, 0, 0), pipeline_mode=pl.Buffered(1))


def _weight_spec(w, layer):
    return _layer_spec(w, layer) if w.ndim == 3 else _const_spec(w.shape)


def _mixer(x, h0, c0, s0, lw, *, NB, C, tiny, layer=0):
    rows, D = x.shape
    W = D // 2
    TB = C * NB
    cx = min(EXACT, C)
    levels = _level_list(C)
    if tiny:
        G = min(SAMPLE_GROUP, NB)
        grid = (NB // G,)
        x_spec = _const_spec((TB, D))
        s_spec = pl.BlockSpec((G, B_HEADS, LANES, LANES), lambda i: (i, 0, 0, 0))
        s0_spec = pl.BlockSpec((None, G, B_HEADS, LANES, LANES), lambda i: (layer, i, 0, 0, 0))
        assert rows == TB
    else:
        G = NB
        grid = (rows // TB,)
        x_spec = pl.BlockSpec((TB, D), lambda i: (i, 0))
        s_spec = s0_spec = _const_spec((NB, B_HEADS, LANES, LANES))
    masks = jnp.asarray(_level_masks(C))
    nl = max(len(levels), 1)
    weights = (lw["ln_mix"], lw["w_in"], lw["conv_w"], lw["conv_b"], lw["w_ra"], lw["b_ra"],
               lw["w_rx"], lw["b_rx"], lw["lam"], lw["lb"], lw["gn"], lw["w_out"], masks)
    in_specs = [x_spec, _const_spec(h0.shape), _const_spec(c0.shape), s0_spec]
    in_specs += [_weight_spec(w, layer) for w in weights[:-1]] + [_const_spec(masks.shape)]
    out_shape = (jax.ShapeDtypeStruct((rows, D), _F32), jax.ShapeDtypeStruct((NB, W), _F32),
                 jax.ShapeDtypeStruct((3, NB, W), _F32),
                 jax.ShapeDtypeStruct((NB, B_HEADS, LANES, LANES), _F32))
    xo_spec = _const_spec((TB, D), single=False) if tiny else x_spec
    so_spec = s_spec if tiny else _const_spec((NB, B_HEADS, LANES, LANES), single=False)
    out_specs = (xo_spec, _const_spec((NB, W), single=False), _const_spec((3, NB, W), single=False), so_spec)
    NS = C // cx
    scratch = [
        pltpu.VMEM((TB, 6 * W), _F32),
        pltpu.VMEM((C + 3, NB, W), _F32),
        pltpu.VMEM((C, NB, W), _F32),
        pltpu.VMEM((C, NB, W), _F32),
        pltpu.VMEM((NS, cx, NB, W), _F32),
        pltpu.VMEM((NS, cx, NB, W), _F32),
        pltpu.VMEM((NS, cx, NB, W), _F32),
        pltpu.VMEM((NS, cx, NB, W), _F32),
        pltpu.VMEM((B_HEADS, TB, LANES), _F32),
        pltpu.VMEM((B_HEADS, TB, LANES), _F32),
        pltpu.VMEM((B_HEADS, TB, LANES), _F32),
        pltpu.VMEM((nl, B_HEADS, TB, LANES), _F32),
        pltpu.VMEM((B_HEADS, TB, LANES), _F32),
        pltpu.VMEM((B_HEADS, NB, LANES), _F32),
        pltpu.VMEM((B_HEADS, LANES, NB if tiny else LANES), _F32),
        pltpu.VMEM((1 if tiny else NB * B_HEADS, C if not tiny else SUBLANES, C if not tiny else LANES), _F32),
    ]
    kern = functools.partial(_mixer_kernel, NB=NB, C=C, cx=cx, levels=levels, tiny=tiny, G=G)
    return pl.pallas_call(
        kern, grid=grid, in_specs=in_specs, out_specs=out_specs, out_shape=out_shape,
        scratch_shapes=scratch,
        compiler_params=pltpu.CompilerParams(dimension_semantics=("arbitrary",),
                                             vmem_limit_bytes=VMEM_LIMIT),
        name="mixer_tiny" if tiny else "mixer_chunk",
    )(x, h0, c0, s0, *weights)


def _mlp_kernel(x_ref, ln_ref, wup_ref, wdn_ref, lnf_ref, o_ref, *, final):
    x = x_ref[...]
    pn = _rms_rows(x, ln_ref[...]).astype(_BF)
    acc = x
    F = wup_ref.shape[1]
    for c0 in range(0, F, MLP_COLS):
        hmid = jnp.dot(pn, wup_ref[:, c0:c0 + MLP_COLS], preferred_element_type=_F32)
        hmid = jnp.square(jnp.maximum(hmid, 0.0)).astype(_BF)
        acc = acc + jnp.dot(hmid, wdn_ref[c0:c0 + MLP_COLS, :], preferred_element_type=_F32)
    if final:
        acc = _rms_rows(acc, lnf_ref[...])
    o_ref[...] = acc


def _mlp(x, lw, lnf, *, final, layer):
    rows, D = x.shape
    TM = min(MLP_ROWS, rows)
    assert rows % TM == 0
    weights = (lw["ln_mlp"], lw["w_up"], lw["w_down"], lnf)
    x_spec = pl.BlockSpec((TM, D), lambda i: (i, 0))
    return pl.pallas_call(
        functools.partial(_mlp_kernel, final=final), grid=(rows // TM,),
        in_specs=[x_spec] + [_weight_spec(w, layer) for w in weights],
        out_specs=x_spec, out_shape=jax.ShapeDtypeStruct((rows, D), _F32),
        compiler_params=pltpu.CompilerParams(dimension_semantics=("arbitrary",),
                                             vmem_limit_bytes=VMEM_LIMIT),
        name="mlp",
    )(x, *weights)


def _block_diag(w):
    H, I, J = w.shape
    eye = jnp.eye(H, dtype=w.dtype)
    return (w[:, :, None, :] * eye[:, None, :, None]).reshape(H * I, H * J)


def kernel(x_prompt, x_sample, state_rglru_h, state_rglru_conv, state_hgrn2, meta_tokens, ln_mix, w_in,
           conv_w, conv_b, w_rg_a, b_rg_a, w_rg_x, b_rg_x, rg_lambda, hgrn_lb, hgrn_norm, w_out, ln_mlp,
           w_up, w_down, ln_final):
    B, T, D = x_prompt.shape
    NS, TS, _ = x_sample.shape
    depth = w_in.shape[0]
    W = D // 2
    assert B == SUBLANES and W == B_HEADS * LANES and T % PROMPT_CHUNK == 0

    lbs = jax.nn.softmax(hgrn_lb.astype(_F32), axis=0)
    lb_all = jnp.clip(jnp.cumsum(lbs, axis=0) - lbs[0], 0.0, 1.0)

    xp = jnp.swapaxes(x_prompt, 0, 1).reshape(T * B, D)
    xs = jnp.swapaxes(x_sample, 0, 1).reshape(TS * NS, D)
    xm = jnp.broadcast_to(meta_tokens.astype(_F32)[:, None, :], (N_META, B, D)).reshape(N_META * B, D)
    conv_s = jnp.swapaxes(state_rglru_conv, 1, 2)
    lnf = ln_final.reshape(1, D)

    zero_h = jnp.zeros((B, W), _F32)
    zero_c = jnp.zeros((CONV_W - 1, B, W), _F32)
    zero_s = jnp.zeros((B, B_HEADS, LANES, LANES), _F32)

    big = dict(w_in=w_in.astype(_BF), w_out=w_out.astype(_BF), w_up=w_up.astype(_BF),
               w_down=w_down.astype(_BF), w_ra=jax.vmap(_block_diag)(w_rg_a).astype(_BF),
               w_rx=jax.vmap(_block_diag)(w_rg_x).astype(_BF))

    outs = {k: [] for k in ("ph", "pc", "pS", "sh", "sc", "sS")}
    for l in range(depth):
        lw = dict(
            big, ln_mix=ln_mix[l].reshape(1, D), conv_w=conv_w[l], conv_b=conv_b[l].reshape(1, W),
            b_ra=b_rg_a[l].reshape(1, W), b_rx=b_rg_x[l].reshape(1, W), lam=rg_lambda[l].reshape(1, W),
            lb=lb_all[l].reshape(1, W), gn=hgrn_norm[l].reshape(1, LANES), ln_mlp=ln_mlp[l].reshape(1, D))
        final = l == depth - 1
        xm, hm, cm, sm = _mixer(xm, zero_h, zero_c, zero_s, lw, NB=B, C=N_META, tiny=False, layer=l)
        if not final:
            xm = _mlp(xm, lw, lnf, final=False, layer=l)
        xp, hp, cp, sp = _mixer(xp, hm, cm, sm, lw, NB=B, C=PROMPT_CHUNK, tiny=False, layer=l)
        xp = _mlp(xp, lw, lnf, final=final, layer=l)
        xs, hs, cs, ss = _mixer(xs, state_rglru_h[l], conv_s[l], state_hgrn2, lw, NB=NS, C=TS, tiny=True,
                                layer=l)
        xs = _mlp(xs, lw, lnf, final=final, layer=l)
        outs["ph"].append(hp)
        outs["pc"].append(jnp.swapaxes(cp, 0, 1))
        outs["pS"].append(jnp.swapaxes(sp, -1, -2))
        outs["sh"].append(hs)
        outs["sc"].append(jnp.swapaxes(cs, 0, 1))
        outs["sS"].append(ss)

    y_prompt = jnp.swapaxes(xp.reshape(T, B, D), 0, 1)
    y_sample = jnp.swapaxes(xs.reshape(TS, NS, D), 0, 1)
    return (y_prompt, y_sample, jnp.stack(outs["ph"]), jnp.stack(outs["pc"]), jnp.stack(outs["pS"]),
            jnp.stack(outs["sh"]), jnp.stack(outs["sc"]), jnp.stack(outs["sS"]))
```

```python
import functools

import jax
import jax.numpy as jnp
import numpy as np
from jax import lax
from jax.experimental import pallas as pl
from jax.experimental.pallas import tpu as pltpu

N_META = 16
CONV_W = 4
A_HEADS = 8
B_HEADS = 4
RG_C = 8.0
EPS = 1e-6
F_MIN = 1e-30
LOG2E = 1.4426950408889634

LANES = 128
SUBLANES = 8
EXACT = 8
PROMPT_CHUNK = 64
SAMPLE_GROUP = 8
MLP_ROWS = 512
MLP_COLS = 1024
VMEM_LIMIT = 56 * 1024 * 1024

_BF = jnp.bfloat16
_F32 = jnp.float32
_NT = (((1,), (1,)), ((), ()))
_TN = (((0,), (0,)), ((), ()))


def _rms_rows(x, g):
    ms = jnp.mean(x * x, axis=-1, keepdims=True)
    return x * lax.rsqrt(ms + EPS) * g


def _sigmoid_pair(x):
    e = jnp.exp(-jnp.abs(x))
    inv = 1.0 / (1.0 + e)
    big, small = inv, e * inv
    pos = x >= 0
    return jnp.where(pos, big, small), jnp.where(pos, small, big)


def _pad_rows(x, rows):
    if x.shape[0] == rows:
        return x
    return jnp.concatenate([x, jnp.zeros((rows - x.shape[0], x.shape[1]), x.dtype)], axis=0)


def _level_list(chunk):
    out, lv = [], EXACT
    while 2 * lv <= chunk:
        out.append(lv)
        lv *= 2
    return tuple(out)


def _level_masks(chunk):
    lv = _level_list(chunk)
    t = np.arange(chunk)[:, None]
    s = np.arange(chunk)[None, :]
    ms = [((t // (2 * L) == s // (2 * L)) & (t % (2 * L) >= L) & (s % (2 * L) < L)) for L in lv]
    if not ms:
        return np.zeros((1, SUBLANES, LANES), np.float32)
    return np.stack(ms).astype(np.float32)


def _mixer_kernel(x_ref, h0_ref, c0_ref, s0_ref, lnm_ref, win_ref, cw_ref, cb_ref, wra_ref,
                  bra_ref, wrx_ref, brx_ref, lam_ref, lb_ref, gn_ref, wout_ref, mask_ref,
                  xo_ref, h_ref, c_ref, s_ref,
                  u_sc, xp_sc, a_sc, hs_sc, q_sc, k_sc, b_sc, od_sc,
                  qe_sc, kb_sc, v_sc, m_sc, o_sc, eb_sc, ebt_sc, am_sc,
                  *, NB, C, cx, levels, tiny, G):
    W = h_ref.shape[-1]
    TB = C * NB
    NS = C // cx
    i = pl.program_id(0)

    def front():
        hn = _rms_rows(x_ref[...], lnm_ref[...]).astype(_BF)

        def proj(g):
            u_sc[:, g * W:(g + 1) * W] = jnp.dot(hn, win_ref[:, g * W:(g + 1) * W],
                                                 preferred_element_type=_F32)

        proj(0)
        proj(3)
        proj(2)
        xp_sc[3:] = u_sc[:, 0:W].reshape(C, NB, W)
        cw = cw_ref[...]
        xc = (cb_ref[...] + xp_sc[3:C + 3] * cw[3:4] + xp_sc[2:C + 2] * cw[2:3]
              + xp_sc[1:C + 1] * cw[1:2] + xp_sc[0:C] * cw[0:1])
        c_ref[...] = xp_sc[C:C + 3]
        xc2 = xc.reshape(TB, W)
        xcb = xc2.astype(_BF)
        r = jax.nn.sigmoid(jnp.dot(xcb, wra_ref[...], preferred_element_type=_F32) + bra_ref[...])
        ig = jax.nn.sigmoid(jnp.dot(xcb, wrx_ref[...], preferred_element_type=_F32) + brx_ref[...])
        proj(4)
        proj(5)
        proj(1)
        lam = lam_ref[...]
        sp = jnp.maximum(-lam, 0.0) + jnp.log1p(jnp.exp(-jnp.abs(lam)))
        a = jnp.exp((-RG_C) * r * sp)
        om = 1.0 - a * a
        bt = jnp.where(om > 0.0, om * lax.rsqrt(om), 0.0) * ig * xc2
        a_sc[...] = a.reshape(C, NB, W)
        hs_sc[...] = bt.reshape(C, NB, W)
        h = h_ref[...]
        for t in range(C):
            h = a_sc[t] * h + hs_sc[t]
            hs_sc[t] = h
        h_ref[...] = h

        lb = lb_ref[...]
        qb = u_sc[:, 2 * W:3 * W]
        q = qb * jax.nn.sigmoid(qb) * (LANES ** -0.5)
        sg, sgn = _sigmoid_pair(u_sc[:, 3 * W:4 * W])
        f = lb + (1.0 - lb) * sg
        k = (1.0 - lb) * sgn
        logf = jnp.log(jnp.maximum(f, F_MIN)) * LOG2E
        q_sc[...] = q.reshape(NS, cx, NB, W)
        k_sc[...] = k.reshape(NS, cx, NB, W)
        b_sc[...] = logf.reshape(NS, cx, NB, W)
        b = jnp.zeros((NB, W), _F32)
        for t in range(C):
            b = b + b_sc[t // cx, t % cx]
            b_sc[t // cx, t % cx] = b
        b_last = b
        vv = u_sc[:, 4 * W:5 * W]
        b2 = b_sc[...].reshape(TB, W)
        q2 = q_sc[...].reshape(TB, W)
        k2 = k_sc[...].reshape(TB, W)
        qe = q2 * jnp.exp2(b2)
        kb = (k2.reshape(C, NB, W) * jnp.exp2(b_last[None] - b2.reshape(C, NB, W))).reshape(TB, W)
        eb = jnp.exp2(b_last)
        for hh in range(B_HEADS):
            sl = slice(hh * LANES, (hh + 1) * LANES)
            qe_sc[hh] = qe[:, sl]
            kb_sc[hh] = kb[:, sl]
            v_sc[hh] = vv[:, sl]
            eb_sc[hh] = eb[:, sl]
            if tiny:
                ebt_sc[hh] = eb[:, sl].T

        for li, L in enumerate(levels):
            spb = (2 * L) // cx
            for blk in range(C // (2 * L)):
                mid = blk * 2 * L + L - 1
                rmid = b_sc[mid // cx, mid % cx][None]
                bb = b_sc[blk * spb:(blk + 1) * spb].reshape(2 * L, NB, W)
                kk = k_sc[blk * spb:(blk + 1) * spb].reshape(2 * L, NB, W)
                qq = q_sc[blk * spb:(blk + 1) * spb].reshape(2 * L, NB, W)
                lo = (kk[:L] * jnp.exp2(rmid - bb[:L])).reshape(L * NB, W)
                hi = (qq[L:] * jnp.exp2(bb[L:] - rmid)).reshape(L * NB, W)
                r0 = blk * 2 * L * NB
                for hh in range(B_HEADS):
                    sl = slice(hh * LANES, (hh + 1) * LANES)
                    m_sc[li, hh, r0:r0 + L * NB, :] = lo[:, sl]
                    m_sc[li, hh, r0 + L * NB:r0 + 2 * L * NB, :] = hi[:, sl]

        v4 = vv.reshape(NS, cx, NB, W)
        for t in range(cx):
            acc = [jnp.zeros((NS, NB, LANES), _F32) for _ in range(B_HEADS)]
            for j in range(t + 1):
                p = q_sc[:, t] * k_sc[:, j]
                if j < t:
                    p = p * jnp.exp2(b_sc[:, t] - b_sc[:, j])
                vj = v4[:, j]
                for hh in range(B_HEADS):
                    sl = slice(hh * LANES, (hh + 1) * LANES)
                    acc[hh] = acc[hh] + jnp.sum(p[..., sl], axis=-1, keepdims=True) * vj[..., sl]
            od_sc[:, t] = jnp.concatenate(acc, axis=-1)

    def chunk_state():
        units = [(bi, hh) for bi in range(NB) for hh in range(B_HEADS)]
        for ui, (bi, hh) in enumerate(units):
            rows = pl.ds(bi, C, stride=NB)
            amat = jnp.zeros((C, C), _F32)
            for li in range(len(levels)):
                m = m_sc[li, hh, rows, :].astype(_BF)
                amat = amat + lax.dot_general(m, m, _NT, preferred_element_type=_F32) * mask_ref[li]
            am_sc[ui] = amat
        for ui, (bi, hh) in enumerate(units):
            rows = pl.ds(bi, C, stride=NB)
            qe = qe_sc[hh, rows, :].astype(_BF)
            v = v_sc[hh, rows, :].astype(_BF)
            o = lax.dot_general(qe, s_ref[bi, hh].astype(_BF), _NT, preferred_element_type=_F32)
            o_sc[hh, rows, :] = o + jnp.dot(am_sc[ui].astype(_BF), v, preferred_element_type=_F32)
        for ui, (bi, hh) in enumerate(units):
            rows = pl.ds(bi, C, stride=NB)
            kb = kb_sc[hh, rows, :].astype(_BF)
            v = v_sc[hh, rows, :].astype(_BF)
            upd = lax.dot_general(v, kb, _TN, preferred_element_type=_F32)
            s_ref[bi, hh] = s_ref[bi, hh] * eb_sc[hh, pl.ds(bi, 1), :] + upd

    def tiny_state():
        lane = lax.broadcasted_iota(jnp.int32, (LANES, NB), 1)

        units = [(j, hh) for j in range(G) for hh in range(B_HEADS)]
        for j, hh in units:
            rows = pl.ds(i * G + j, C, stride=NB)
            qe = _pad_rows(qe_sc[hh, rows, :], 2 * SUBLANES).astype(_BF)
            o = jnp.dot(qe, s0_ref[j, hh].astype(_BF), preferred_element_type=_F32)
            o_sc[hh, rows, :] = o[:C]
        for j, hh in units:
            n = i * G + j
            rows = pl.ds(n, C, stride=NB)
            kb = _pad_rows(kb_sc[hh, rows, :], 2 * SUBLANES).astype(_BF)
            v = _pad_rows(v_sc[hh, rows, :], 2 * SUBLANES).astype(_BF)
            ebcol = jnp.sum(jnp.where(lane == n, ebt_sc[hh], 0.0), axis=1, keepdims=True)
            upd = lax.dot_general(kb, v, _TN, preferred_element_type=_F32)
            s_ref[j, hh] = s0_ref[j, hh] * ebcol + upd

    def back():
        og = []
        od = od_sc[...].reshape(TB, W)
        for hh in range(B_HEADS):
            sl = slice(hh * LANES, (hh + 1) * LANES)
            o = o_sc[hh] + od[:, sl]
            gb = u_sc[:, 5 * W + hh * LANES:5 * W + (hh + 1) * LANES]
            ms = jnp.mean(o * o, axis=-1, keepdims=True)
            og.append(o * lax.rsqrt(ms + EPS) * gn_ref[...] * (gb * jax.nn.sigmoid(gb)))
        ya = hs_sc[...].reshape(TB, W) * jax.nn.gelu(u_sc[:, W:2 * W])
        y = jnp.concatenate([ya] + og, axis=-1).astype(_BF)
        xo_ref[...] = x_ref[...] + jnp.dot(y, wout_ref[...], preferred_element_type=_F32)

    if tiny:
        @pl.when(i == 0)
        def _():
            h_ref[...] = h0_ref[...]
            xp_sc[0:3] = c0_ref[...]
            front()

        tiny_state()

        @pl.when(i == pl.num_programs(0) - 1)
        def _():
            back()
    else:
        @pl.when(i == 0)
        def _():
            h_ref[...] = h0_ref[...]
            c_ref[...] = c0_ref[...]
            s_ref[...] = s0_ref[...]

        xp_sc[0:3] = c_ref[...]
        front()
        chunk_state()
        back()


def _const_spec(shape, single=True):
    nd = len(shape)
    if single:
        return pl.BlockSpec(shape, lambda i, _n=nd: (0,) * _n, pipeline_mode=pl.Buffered(1))
    return pl.BlockSpec(shape, lambda i, _n=nd: (0,) * _n)


def _layer_spec(arr, layer):
    return pl.BlockSpec((None,) + arr.shape[1:], lambda i: (layer, 0, 0), pipeline_mode=pl.Buffered(1))


def _weight_spec(w, layer):
    return _layer_spec(w, layer) if w.ndim == 3 else _const_spec(w.shape)


def _mixer(x, h0, c0, s0, lw, *, NB, C, tiny, layer=0):
    rows, D = x.shape
    W = D // 2
    TB = C * NB
    cx = min(EXACT, C)
    levels = _level_list(C)
    if tiny:
        G = min(SAMPLE_GROUP, NB)
        grid = (NB // G,)
        x_spec = _const_spec((TB, D))
        s_spec = pl.BlockSpec((G, B_HEADS, LANES, LANES), lambda i: (i, 0, 0, 0))
        s0_spec = pl.BlockSpec((None, G, B_HEADS, LANES, LANES), lambda i: (layer, i, 0, 0, 0))
        assert rows == TB
    else:
        G = NB
        grid = (rows // TB,)
        x_spec = pl.BlockSpec((TB, D), lambda i: (i, 0))
        s_spec = s0_spec = _const_spec((NB, B_HEADS, LANES, LANES))
    masks = jnp.asarray(_level_masks(C))
    nl = max(len(levels), 1)
    weights = (lw["ln_mix"], lw["w_in"], lw["conv_w"], lw["conv_b"], lw["w_ra"], lw["b_ra"],
               lw["w_rx"], lw["b_rx"], lw["lam"], lw["lb"], lw["gn"], lw["w_out"], masks)
    in_specs = [x_spec, _const_spec(h0.shape), _const_spec(c0.shape), s0_spec]
    in_specs += [_weight_spec(w, layer) for w in weights[:-1]] + [_const_spec(masks.shape)]
    out_shape = (jax.ShapeDtypeStruct((rows, D), _F32), jax.ShapeDtypeStruct((NB, W), _F32),
                 jax.ShapeDtypeStruct((3, NB, W), _F32),
                 jax.ShapeDtypeStruct((NB, B_HEADS, LANES, LANES), _F32))
    xo_spec = _const_spec((TB, D), single=False) if tiny else x_spec
    so_spec = s_spec if tiny else _const_spec((NB, B_HEADS, LANES, LANES), single=False)
    out_specs = (xo_spec, _const_spec((NB, W), single=False), _const_spec((3, NB, W), single=False), so_spec)
    NS = C // cx
    scratch = [
        pltpu.VMEM((TB, 6 * W), _F32),
        pltpu.VMEM((C + 3, NB, W), _F32),
        pltpu.VMEM((C, NB, W), _F32),
        pltpu.VMEM((C, NB, W), _F32),
        pltpu.VMEM((NS, cx, NB, W), _F32),
        pltpu.VMEM((NS, cx, NB, W), _F32),
        pltpu.VMEM((NS, cx, NB, W), _F32),
        pltpu.VMEM((NS, cx, NB, W), _F32),
        pltpu.VMEM((B_HEADS, TB, LANES), _F32),
        pltpu.VMEM((B_HEADS, TB, LANES), _F32),
        pltpu.VMEM((B_HEADS, TB, LANES), _F32),
        pltpu.VMEM((nl, B_HEADS, TB, LANES), _F32),
        pltpu.VMEM((B_HEADS, TB, LANES), _F32),
        pltpu.VMEM((B_HEADS, NB, LANES), _F32),
        pltpu.VMEM((B_HEADS, LANES, NB if tiny else LANES), _F32),
        pltpu.VMEM((1 if tiny else NB * B_HEADS, C if not tiny else SUBLANES, C if not tiny else LANES), _F32),
    ]
    kern = functools.partial(_mixer_kernel, NB=NB, C=C, cx=cx, levels=levels, tiny=tiny, G=G)
    return pl.pallas_call(
        kern, grid=grid, in_specs=in_specs, out_specs=out_specs, out_shape=out_shape,
        scratch_shapes=scratch,
        compiler_params=pltpu.CompilerParams(dimension_semantics=("arbitrary",),
                                             vmem_limit_bytes=VMEM_LIMIT),
        name="mixer_tiny" if tiny else "mixer_chunk",
    )(x, h0, c0, s0, *weights)


def _mlp_kernel(x_ref, ln_ref, wup_ref, wdn_ref, lnf_ref, o_ref, *scratch, final, n_seq):
    x = x_ref[...]
    pn = _rms_rows(x, ln_ref[...]).astype(_BF)
    acc = x
    F = wup_ref.shape[1]
    for c0 in range(0, F, MLP_COLS):
        hmid = jnp.dot(pn, wup_ref[:, c0:c0 + MLP_COLS], preferred_element_type=_F32)
        hmid = jnp.square(jnp.maximum(hmid, 0.0)).astype(_BF)
        acc = acc + jnp.dot(hmid, wdn_ref[c0:c0 + MLP_COLS, :], preferred_element_type=_F32)
    if final:
        acc = _rms_rows(acc, lnf_ref[...])
    if n_seq is None:
        o_ref[...] = acc
    else:
        (t_sc,) = scratch
        steps = acc.shape[0] // n_seq
        for j in range(acc.shape[1] // LANES):
            t_sc[j] = acc[:, j * LANES:(j + 1) * LANES]
        for b in range(n_seq):
            for j in range(acc.shape[1] // LANES):
                o_ref[b, :, j * LANES:(j + 1) * LANES] = t_sc[j, pl.ds(b, steps, stride=n_seq), :]


def _mlp(x, lw, lnf, *, final, layer, n_seq=None):
    rows, D = x.shape
    TM = min(MLP_ROWS, rows)
    assert rows % TM == 0
    weights = (lw["ln_mlp"], lw["w_up"], lw["w_down"], lnf)
    x_spec = pl.BlockSpec((TM, D), lambda i: (i, 0))
    if n_seq is None:
        out_spec, out_shape, scratch = x_spec, jax.ShapeDtypeStruct((rows, D), _F32), []
    else:
        out_spec = pl.BlockSpec((n_seq, TM // n_seq, D), lambda i: (0, i, 0))
        out_shape = jax.ShapeDtypeStruct((n_seq, rows // n_seq, D), _F32)
        scratch = [pltpu.VMEM((D // LANES, TM, LANES), _F32)]
    return pl.pallas_call(
        functools.partial(_mlp_kernel, final=final, n_seq=n_seq), grid=(rows // TM,),
        in_specs=[x_spec] + [_weight_spec(w, layer) for w in weights],
        out_specs=out_spec, out_shape=out_shape, scratch_shapes=scratch,
        compiler_params=pltpu.CompilerParams(dimension_semantics=("arbitrary",),
                                             vmem_limit_bytes=VMEM_LIMIT),
        name="mlp",
    )(x, *weights)


def _block_diag(w):
    H, I, J = w.shape
    eye = jnp.eye(H, dtype=w.dtype)
    return (w[:, :, None, :] * eye[:, None, :, None]).reshape(H * I, H * J)


def kernel(x_prompt, x_sample, state_rglru_h, state_rglru_conv, state_hgrn2, meta_tokens, ln_mix, w_in,
           conv_w, conv_b, w_rg_a, b_rg_a, w_rg_x, b_rg_x, rg_lambda, hgrn_lb, hgrn_norm, w_out, ln_mlp,
           w_up, w_down, ln_final):
    B, T, D = x_prompt.shape
    NS, TS, _ = x_sample.shape
    depth = w_in.shape[0]
    W = D // 2
    assert B == SUBLANES and W == B_HEADS * LANES and T % PROMPT_CHUNK == 0

    lbs = jax.nn.softmax(hgrn_lb.astype(_F32), axis=0)
    lb_all = jnp.clip(jnp.cumsum(lbs, axis=0) - lbs[0], 0.0, 1.0)

    xp = jnp.swapaxes(x_prompt, 0, 1).reshape(T * B, D)
    xs = jnp.swapaxes(x_sample, 0, 1).reshape(TS * NS, D)
    xm = jnp.broadcast_to(meta_tokens.astype(_F32)[:, None, :], (N_META, B, D)).reshape(N_META * B, D)
    conv_s = jnp.swapaxes(state_rglru_conv, 1, 2)
    lnf = ln_final.reshape(1, D)

    zero_h = jnp.zeros((B, W), _F32)
    zero_c = jnp.zeros((CONV_W - 1, B, W), _F32)
    zero_s = jnp.zeros((B, B_HEADS, LANES, LANES), _F32)

    big = dict(w_in=w_in.astype(_BF), w_out=w_out.astype(_BF), w_up=w_up.astype(_BF),
               w_down=w_down.astype(_BF), w_ra=jax.vmap(_block_diag)(w_rg_a).astype(_BF),
               w_rx=jax.vmap(_block_diag)(w_rg_x).astype(_BF))

    outs = {k: [] for k in ("ph", "pc", "pS", "sh", "sc", "sS")}
    for l in range(depth):
        lw = dict(
            big, ln_mix=ln_mix[l].reshape(1, D), conv_w=conv_w[l], conv_b=conv_b[l].reshape(1, W),
            b_ra=b_rg_a[l].reshape(1, W), b_rx=b_rg_x[l].reshape(1, W), lam=rg_lambda[l].reshape(1, W),
            lb=lb_all[l].reshape(1, W), gn=hgrn_norm[l].reshape(1, LANES), ln_mlp=ln_mlp[l].reshape(1, D))
        final = l == depth - 1
        xm, hm, cm, sm = _mixer(xm, zero_h, zero_c, zero_s, lw, NB=B, C=N_META, tiny=False, layer=l)
        if not final:
            xm = _mlp(xm, lw, lnf, final=False, layer=l)
        xp, hp, cp, sp = _mixer(xp, hm, cm, sm, lw, NB=B, C=PROMPT_CHUNK, tiny=False, layer=l)
        xp = _mlp(xp, lw, lnf, final=final, layer=l, n_seq=B if final else None)
        xs, hs, cs, ss = _mixer(xs, state_rglru_h[l], conv_s[l], state_hgrn2, lw, NB=NS, C=TS, tiny=True,
                                layer=l)
        xs = _mlp(xs, lw, lnf, final=final, layer=l)
        outs["ph"].append(hp)
        outs["pc"].append(jnp.swapaxes(cp, 0, 1))
        outs["pS"].append(jnp.swapaxes(sp, -1, -2))
        outs["sh"].append(hs)
        outs["sc"].append(jnp.swapaxes(cs, 0, 1))
        outs["sS"].append(ss)

    y_prompt = xp
    y_sample = jnp.swapaxes(xs.reshape(TS, NS, D), 0, 1)
    return (y_prompt, y_sample, jnp.stack(outs["ph"]), jnp.stack(outs["pc"]), jnp.stack(outs["pS"]),
            jnp.stack(outs["sh"]), jnp.stack(outs["sc"]), jnp.stack(outs["sS"]))
```
